```python
import math
import jax
import jax.numpy as jnp
from jax import lax
import numpy as np

D_MODEL = 1024
BATCH = 2
SEQ = 16384
DEPTH = 2

CHUNK = 64
N_BRANCH = 4
MIX_WIDTH = D_MODEL // 4
CONV_W = 4
EPS = 1e-6

GDN_HEADS = 4
GDN_HEAD_DIM = MIX_WIDTH // GDN_HEADS

LRU_BLOCKS = 4
LRU_BLOCK = MIX_WIDTH // LRU_BLOCKS
LRU_C = 8.0

S5_GROUP = 16
S5_GROUPS = MIX_WIDTH // S5_GROUP
S5_STATE = 64

RWKV_HEADS = 4
RWKV_HEAD_DIM = MIX_WIDTH // RWKV_HEADS
DECAY_LORA = 64
AAA_LORA = 64
GATE_LORA = 128
RWKV_LN_EPS = 64e-5

D_FF = 4 * D_MODEL

GDN_COLS = 4 * MIX_WIDTH + 2 * GDN_HEADS
LRU_COLS = 2 * MIX_WIDTH
S5_COLS = MIX_WIDTH
RWKV_COLS = 3 * MIX_WIDTH + DECAY_LORA + AAA_LORA + GATE_LORA
GATE_COLS = N_BRANCH * D_MODEL
D_IN = GDN_COLS + LRU_COLS + S5_COLS + RWKV_COLS + GATE_COLS

kernel_name = "hybrid_gdn_rglru_s5_rwkv7_block"


def _split(t, widths):
    parts, off = [], 0
    for w in widths:
        parts.append(t[..., off:off + w])
        off += w
    return parts


def _rmsnorm(x, g):
    x32 = x.astype(jnp.float32)
    y = x32 * lax.rsqrt(jnp.mean(x32 * x32, axis=-1, keepdims=True) + EPS)
    return (y * g.astype(jnp.float32)).astype(x.dtype)


def _l2norm(t):
    return t * lax.rsqrt(jnp.sum(t * t, axis=-1, keepdims=True) + EPS)


def _causal_dwconv(u, w):
    k, c = w.shape
    return lax.conv_general_dilated(
        u, w[:, None, :].astype(u.dtype), window_strides=(1,), padding=[(k - 1, 0)],
        dimension_numbers=("NWC", "WIO", "NWC"), feature_group_count=c)


def _linear_recurrence(a, b):
    def combine(left, right):
        return left[0] * right[0], right[0] * left[1] + right[1]
    return lax.associative_scan(combine, (a, b), axis=1)[1]


def _chunk_gated_delta(q, k, v, g, beta):
    b, s, h, dk = q.shape
    dv = v.shape[-1]
    n = s // CHUNK

    def to_chunks(t):
        t = t.reshape((b, n, CHUNK, h) + t.shape[3:])
        return jnp.moveaxis(t, 3, 1)

    q, k, v, g, beta = (to_chunks(t) for t in (q, k, v, g, beta))
    gc = jnp.cumsum(g, axis=-1)
    idx = jnp.arange(CHUNK)
    causal = idx[:, None] >= idx[None, :]
    strict = idx[:, None] > idx[None, :]
    diff = gc[..., :, None] - gc[..., None, :]
    decay = jnp.where(causal, jnp.exp(jnp.where(causal, diff, 0.0)), 0.0)
    kb = k * beta[..., None]
    lmat = jnp.where(strict, jnp.einsum("bhnid,bhnjd->bhnij", kb, k) * decay, 0.0)
    eye = jnp.eye(CHUNK, dtype=q.dtype)
    rhs = jnp.concatenate([v * beta[..., None], kb * jnp.exp(gc)[..., None]], axis=-1)
    sol = lax.linalg.triangular_solve(eye + lmat, rhs, left_side=True, lower=True)
    u, w = sol[..., :dv], sol[..., dv:]
    attn = jnp.einsum("bhnid,bhnjd->bhnij", q, k) * decay
    q_dec = q * jnp.exp(gc)[..., None]
    g_last = gc[..., -1]
    k_dec = k * jnp.exp(g_last[..., None] - gc)[..., None]

    def step(state, inp):
        u_i, w_i, q_i, k_i, a_i, gl_i = inp
        v_new = u_i - jnp.einsum("bhck,bhkv->bhcv", w_i, state)
        o = jnp.einsum("bhck,bhkv->bhcv", q_i, state) + jnp.einsum("bhcj,bhjv->bhcv", a_i, v_new)
        state = state * jnp.exp(gl_i)[..., None, None] + jnp.einsum("bhck,bhcv->bhkv", k_i, v_new)
        return state, o

    xs = tuple(jnp.moveaxis(t, 2, 0) for t in (u, w, q_dec, k_dec, attn, g_last))
    state0 = jnp.zeros((b, h, dk, dv), q.dtype)
    _, o = lax.scan(step, state0, xs)
    o = jnp.moveaxis(o, 0, 2)
    return jnp.moveaxis(o, 1, 3).reshape(b, s, h, dv)


def _gdn_mixer(p, conv_w, a_log, dt_bias, norm_g):
    b, s, _ = p.shape
    f32 = jnp.float32
    qkv, z, beta_logit, a_logit = _split(p, (3 * MIX_WIDTH, MIX_WIDTH, GDN_HEADS, GDN_HEADS))
    qkv = jax.nn.silu(_causal_dwconv(qkv, conv_w)).astype(f32)
    q, k, v = (t.reshape(b, s, GDN_HEADS, GDN_HEAD_DIM) for t in _split(qkv, (MIX_WIDTH,) * 3))
    q = _l2norm(q) * (GDN_HEAD_DIM ** -0.5)
    k = _l2norm(k)
    beta = jax.nn.sigmoid(beta_logit.astype(f32))
    g = -jnp.exp(a_log.astype(f32)) * jax.nn.softplus(a_logit.astype(f32) + dt_bias.astype(f32))
    o = _chunk_gated_delta(q, k, v, g, beta)
    o = o * lax.rsqrt(jnp.mean(o * o, axis=-1, keepdims=True) + EPS) * norm_g.astype(f32)
    o = o * jax.nn.silu(z.astype(f32)).reshape(b, s, GDN_HEADS, GDN_HEAD_DIM)
    return o.reshape(b, s, MIX_WIDTH).astype(p.dtype)


def _rglru_mixer(p, conv_w, conv_b, w_a, b_a, w_x, b_x, lam):
    b, s, _ = p.shape
    f32 = jnp.float32
    xb, gate = _split(p, (MIX_WIDTH, MIX_WIDTH))
    u = (_causal_dwconv(xb, conv_w) + conv_b).astype(f32)
    ub = u.reshape(b, s, LRU_BLOCKS, LRU_BLOCK)
    r = jax.nn.sigmoid(jnp.einsum("bsnd,nde->bsne", ub, w_a.astype(f32)).reshape(b, s, MIX_WIDTH) + b_a.astype(f32))
    i = jax.nn.sigmoid(jnp.einsum("bsnd,nde->bsne", ub, w_x.astype(f32)).reshape(b, s, MIX_WIDTH) + b_x.astype(f32))
    log_a = -LRU_C * r * jax.nn.softplus(-lam.astype(f32))
    a = jnp.exp(log_a)
    inp = jnp.sqrt(-jnp.expm1(2.0 * log_a)) * (i * u)
    h = _linear_recurrence(a, inp)
    return (h * jax.nn.gelu(gate.astype(f32))).astype(p.dtype)


def _s5_mixer(u, lam_re, lam_im, b_re, b_im, c_re, c_im, d, log_dt, glu_w, glu_b):
    b, s, _ = u.shape
    f32 = jnp.float32
    ug = u.astype(f32).reshape(b, s, S5_GROUPS, S5_GROUP)
    lam = lax.complex(lam_re.astype(f32), lam_im.astype(f32))
    dt = jnp.exp(log_dt.astype(f32))[:, None]
    lam_bar = jnp.exp(lam * dt)
    b_bar = ((lam_bar - 1.0) / lam)[..., None] * lax.complex(b_re.astype(f32), b_im.astype(f32))
    bu = jnp.einsum("gpc,bsgc->bsgp", b_bar, ug.astype(jnp.complex64))
    h = _linear_recurrence(jnp.broadcast_to(lam_bar, bu.shape), bu)
    c = lax.complex(c_re.astype(f32), c_im.astype(f32))
    y = jnp.einsum("gcp,bsgp->bsgc", c, h).real + d.astype(f32).reshape(S5_GROUPS, S5_GROUP) * ug
    y = jax.nn.gelu(y.reshape(b, s, MIX_WIDTH))
    y = y * jax.nn.sigmoid(y @ glu_w.astype(f32) + glu_b.astype(f32))
    return y.astype(u.dtype)


def _rwkv7_mixer(p, mu, w0, w_up, a0, a_up, g_up, k_k, k_a, r_k, ln_g, ln_b):
    b, s, _ = p.shape
    f32 = jnp.float32
    dtype = p.dtype
    p = p.astype(f32)
    prev = jnp.pad(p, ((0, 0), (1, 0), (0, 0)))[:, :-1]
    p = p + mu.astype(f32) * (prev - p)
    r, k, v, wd, ad, gd = _split(p, (MIX_WIDTH,) * 3 + (DECAY_LORA, AAA_LORA, GATE_LORA))
    logw = -jax.nn.softplus(-(w0.astype(f32) + jnp.tanh(wd) @ w_up.astype(f32))) - 0.5
    decay = jnp.exp(-jnp.exp(logw))
    a = jax.nn.sigmoid(a0.astype(f32) + ad @ a_up.astype(f32))
    g = jax.nn.sigmoid(gd) @ g_up.astype(f32)

    def heads(t):
        return t.reshape(b, s, RWKV_HEADS, RWKV_HEAD_DIM)

    kk = _l2norm(heads(k * k_k.astype(f32)))
    k = k * (1.0 + (a - 1.0) * k_a.astype(f32))
    r, k, v, decay, a = heads(r), heads(k), heads(v), heads(decay), heads(a)

    def step(state, inp):
        r_t, w_t, k_t, v_t, kk_t, a_t = inp
        removed = jnp.einsum("bhvk,bhk->bhv", state, kk_t)
        state = (state * w_t[:, :, None, :]
                 - jnp.einsum("bhv,bhk->bhvk", removed, kk_t * a_t)
                 + jnp.einsum("bhv,bhk->bhvk", v_t, k_t))
        return state, jnp.einsum("bhvk,bhk->bhv", state, r_t)

    xs = tuple(jnp.moveaxis(t, 1, 0) for t in (r, decay, k, v, kk, a))
    state0 = jnp.zeros((b, RWKV_HEADS, RWKV_HEAD_DIM, RWKV_HEAD_DIM), f32)
    _, o = lax.scan(step, state0, xs)
    o = jnp.moveaxis(o, 0, 1)
    mean = jnp.mean(o, axis=-1, keepdims=True)
    var = jnp.mean(jnp.square(o - mean), axis=-1, keepdims=True)
    o = ((o - mean) * lax.rsqrt(var + RWKV_LN_EPS)).reshape(b, s, MIX_WIDTH)
    o = o * ln_g.astype(f32) + ln_b.astype(f32)
    bonus = jnp.sum(r * k * r_k.astype(f32), axis=-1, keepdims=True) * v
    o = (o + bonus.reshape(b, s, MIX_WIDTH)) * g
    return o.astype(dtype)


def _hybrid_layer(x, norm1_g, w_in,
                  gdn_conv_w, gdn_a_log, gdn_dt_bias, gdn_norm_g,
                  lru_conv_w, lru_conv_b, lru_w_a, lru_b_a, lru_w_x, lru_b_x, lru_lambda,
                  s5_lambda_re, s5_lambda_im, s5_b_re, s5_b_im, s5_c_re, s5_c_im,
                  s5_d, s5_log_dt, s5_glu_w, s5_glu_b,
                  rwkv_mu, rwkv_w0, rwkv_w_up, rwkv_a0, rwkv_a_up, rwkv_g_up,
                  rwkv_k_k, rwkv_k_a, rwkv_r_k, rwkv_ln_g, rwkv_ln_b,
                  w_branch, w_out, norm2_g, mlp_w1, mlp_w2):
    b, s, _ = x.shape
    h = _rmsnorm(x, norm1_g)
    proj = h @ w_in
    p_gdn, p_lru, p_s5, p_rwkv, p_gate = _split(proj, (GDN_COLS, LRU_COLS, S5_COLS, RWKV_COLS, GATE_COLS))
    ys = (
        _gdn_mixer(p_gdn, gdn_conv_w, gdn_a_log, gdn_dt_bias, gdn_norm_g),
        _rglru_mixer(p_lru, lru_conv_w, lru_conv_b, lru_w_a, lru_b_a, lru_w_x, lru_b_x, lru_lambda),
        _s5_mixer(p_s5, s5_lambda_re, s5_lambda_im, s5_b_re, s5_b_im, s5_c_re, s5_c_im,
                  s5_d, s5_log_dt, s5_glu_w, s5_glu_b),
        _rwkv7_mixer(p_rwkv, rwkv_mu, rwkv_w0, rwkv_w_up, rwkv_a0, rwkv_a_up, rwkv_g_up,
                     rwkv_k_k, rwkv_k_a, rwkv_r_k, rwkv_ln_g, rwkv_ln_b),
    )
    gate_logits = p_gate.reshape(b, s, N_BRANCH, D_MODEL)
    merged = None
    for i in range(N_BRANCH):
        term = jax.nn.sigmoid(gate_logits[:, :, i]) * (ys[i] @ w_branch[i])
        merged = term if merged is None else merged + term
    x = x + (merged @ w_out).astype(x.dtype)
    h2 = _rmsnorm(x, norm2_g)
    x = x + (jnp.square(jax.nn.relu(h2 @ mlp_w1)) @ mlp_w2).astype(x.dtype)
    return x


def setup_inputs(seed: int = 0) -> dict:
    key = jax.random.key(seed)
    keys = iter(jax.random.split(key, 64))
    f32 = jnp.float32

    def nrm(shape, scale):
        return scale * jax.random.normal(next(keys), shape, f32)

    def uni(shape, lo, hi):
        return jax.random.uniform(next(keys), shape, f32, lo, hi)

    L, W = DEPTH, MIX_WIDTH
    x = nrm((BATCH, SEQ, D_MODEL), 1.0)
    norm1_g = 1.0 + nrm((L, D_MODEL), 0.02)
    w_in = nrm((L, D_MODEL, D_IN), D_MODEL ** -0.5)
    gdn_conv_w = nrm((L, CONV_W, 3 * W), CONV_W ** -0.5)
    gdn_a_log = jnp.log(uni((L, GDN_HEADS), 1.0, 16.0))
    gdn_dt = jnp.exp(uni((L, GDN_HEADS), math.log(1e-3), math.log(1e-1)))
    gdn_dt_bias = gdn_dt + jnp.log(-jnp.expm1(-gdn_dt))
    gdn_norm_g = 1.0 + nrm((L, GDN_HEAD_DIM), 0.02)
    lru_conv_w = nrm((L, CONV_W, W), CONV_W ** -0.5)
    lru_conv_b = nrm((L, W), 0.01)
    lru_w_a = nrm((L, LRU_BLOCKS, LRU_BLOCK, LRU_BLOCK), LRU_BLOCK ** -0.5)
    lru_b_a = nrm((L, W), 0.01)
    lru_w_x = nrm((L, LRU_BLOCKS, LRU_BLOCK, LRU_BLOCK), LRU_BLOCK ** -0.5)
    lru_b_x = nrm((L, W), 0.01)
    a_base = uni((L, W), 0.9, 0.999) ** (1.0 / LRU_C)
    lru_lambda = jnp.log(a_base) - jnp.log1p(-a_base)
    s5_lambda_re = -0.5 + nrm((L, S5_GROUPS, S5_STATE), 0.01)
    s5_lambda_im = jnp.pi * jnp.arange(S5_STATE, dtype=f32) + nrm((L, S5_GROUPS, S5_STATE), 0.01)
    s5_b_re = nrm((L, S5_GROUPS, S5_STATE, S5_GROUP), (2 * S5_GROUP) ** -0.5)
    s5_b_im = nrm((L, S5_GROUPS, S5_STATE, S5_GROUP), (2 * S5_GROUP) ** -0.5)
    s5_c_re = nrm((L, S5_GROUPS, S5_GROUP, S5_STATE), S5_STATE ** -0.5)
    s5_c_im = nrm((L, S5_GROUPS, S5_GROUP, S5_STATE), S5_STATE ** -0.5)
    s5_d = nrm((L, W), 1.0)
    s5_log_dt = uni((L, S5_GROUPS), math.log(1e-3), math.log(1e-1))
    s5_glu_w = nrm((L, W, W), W ** -0.5)
    s5_glu_b = nrm((L, W), 0.01)
    rwkv_mu = uni((L, RWKV_COLS), 0.0, 1.0)
    rwkv_w0 = uni((L, W), -6.0, -1.0)
    rwkv_w_up = nrm((L, DECAY_LORA, W), 0.1)
    rwkv_a0 = nrm((L, W), 0.1)
    rwkv_a_up = nrm((L, AAA_LORA, W), 0.1)
    rwkv_g_up = nrm((L, GATE_LORA, W), GATE_LORA ** -0.5)
    rwkv_k_k = 0.85 + nrm((L, W), 0.02)
    rwkv_k_a = 1.0 + nrm((L, W), 0.02)
    rwkv_r_k = nrm((L, RWKV_HEADS, RWKV_HEAD_DIM), 0.1)
    rwkv_ln_g = 1.0 + nrm((L, W), 0.02)
    rwkv_ln_b = nrm((L, W), 0.01)
    w_branch = nrm((L, N_BRANCH, W, D_MODEL), W ** -0.5)
    w_out = nrm((L, D_MODEL, D_MODEL), D_MODEL ** -0.5)
    norm2_g = 1.0 + nrm((L, D_MODEL), 0.02)
    mlp_w1 = nrm((L, D_MODEL, D_FF), D_MODEL ** -0.5)
    mlp_w2 = nrm((L, D_FF, D_MODEL), D_FF ** -0.5)
    final_norm_g = 1.0 + nrm((D_MODEL,), 0.02)
    return {
        "x": x, "norm1_g": norm1_g, "w_in": w_in,
        "gdn_conv_w": gdn_conv_w, "gdn_a_log": gdn_a_log, "gdn_dt_bias": gdn_dt_bias, "gdn_norm_g": gdn_norm_g,
        "lru_conv_w": lru_conv_w, "lru_conv_b": lru_conv_b, "lru_w_a": lru_w_a, "lru_b_a": lru_b_a,
        "lru_w_x": lru_w_x, "lru_b_x": lru_b_x, "lru_lambda": lru_lambda,
        "s5_lambda_re": s5_lambda_re, "s5_lambda_im": s5_lambda_im, "s5_b_re": s5_b_re, "s5_b_im": s5_b_im,
        "s5_c_re": s5_c_re, "s5_c_im": s5_c_im, "s5_d": s5_d, "s5_log_dt": s5_log_dt,
        "s5_glu_w": s5_glu_w, "s5_glu_b": s5_glu_b,
        "rwkv_mu": rwkv_mu, "rwkv_w0": rwkv_w0, "rwkv_w_up": rwkv_w_up, "rwkv_a0": rwkv_a0,
        "rwkv_a_up": rwkv_a_up, "rwkv_g_up": rwkv_g_up, "rwkv_k_k": rwkv_k_k, "rwkv_k_a": rwkv_k_a,
        "rwkv_r_k": rwkv_r_k, "rwkv_ln_g": rwkv_ln_g, "rwkv_ln_b": rwkv_ln_b,
        "w_branch": w_branch, "w_out": w_out, "norm2_g": norm2_g, "mlp_w1": mlp_w1, "mlp_w2": mlp_w2,
        "final_norm_g": final_norm_g,
    }


def reference(x, norm1_g, w_in,
              gdn_conv_w, gdn_a_log, gdn_dt_bias, gdn_norm_g,
              lru_conv_w, lru_conv_b, lru_w_a, lru_b_a, lru_w_x, lru_b_x, lru_lambda,
              s5_lambda_re, s5_lambda_im, s5_b_re, s5_b_im, s5_c_re, s5_c_im,
              s5_d, s5_log_dt, s5_glu_w, s5_glu_b,
              rwkv_mu, rwkv_w0, rwkv_w_up, rwkv_a0, rwkv_a_up, rwkv_g_up,
              rwkv_k_k, rwkv_k_a, rwkv_r_k, rwkv_ln_g, rwkv_ln_b,
              w_branch, w_out, norm2_g, mlp_w1, mlp_w2, final_norm_g):
    for l in range(DEPTH):
        x = _hybrid_layer(
            x, norm1_g[l], w_in[l],
            gdn_conv_w[l], gdn_a_log[l], gdn_dt_bias[l], gdn_norm_g[l],
            lru_conv_w[l], lru_conv_b[l], lru_w_a[l], lru_b_a[l], lru_w_x[l], lru_b_x[l], lru_lambda[l],
            s5_lambda_re[l], s5_lambda_im[l], s5_b_re[l], s5_b_im[l], s5_c_re[l], s5_c_im[l],
            s5_d[l], s5_log_dt[l], s5_glu_w[l], s5_glu_b[l],
            rwkv_mu[l], rwkv_w0[l], rwkv_w_up[l], rwkv_a0[l], rwkv_a_up[l], rwkv_g_up[l],
            rwkv_k_k[l], rwkv_k_a[l], rwkv_r_k[l], rwkv_ln_g[l], rwkv_ln_b[l],
            w_branch[l], w_out[l], norm2_g[l], mlp_w1[l], mlp_w2[l])
    return _rmsnorm(x, final_norm_g)
```

```python
import functools
import math

import jax
import jax.numpy as jnp
from jax import lax
from jax.experimental import pallas as pl
from jax.experimental.pallas import tpu as pltpu

F32 = jnp.float32
BF16 = jnp.bfloat16

D_MODEL = 1024
MIX = 256
HEADS = 4
HEAD_DIM = 64
CHUNK = 64
CONV_W = 4
EPS = 1e-6
LRU_C = 8.0
S5_GROUPS = 16
S5_GROUP = 16
S5_STATE = 64
S5_LANES = S5_GROUPS * S5_STATE
DECAY_LORA = 64
AAA_LORA = 64
GATE_LORA = 128
RWKV_LN_EPS = 64e-5
D_FF = 4 * D_MODEL
LANE = 128
CONV_TAIL = 8

GDN_COLS = 4 * MIX + 2 * HEADS
LRU_COLS = 2 * MIX
S5_COLS = MIX
RWKV_COLS = 3 * MIX + DECAY_LORA + AAA_LORA + GATE_LORA
MIX_COLS = 4 * MIX + LRU_COLS + S5_COLS + RWKV_COLS + LANE

ROW_TILE = 512
TC_GDN = 128
TC_RWKV = 128
TC_LRU = 256
TC_S5 = 256
VMEM_LIMIT = 56 * 1024 * 1024

_NN = (((1,), (0,)), ((), ()))
_NT = (((1,), (1,)), ((), ()))
_TN = (((0,), (0,)), ((), ()))


def _mm(a, b, dims=_NN):
    return lax.dot_general(a, b, dims, preferred_element_type=F32)


def _mm1(a, b, dims=_NN):
    return _mm(a.astype(BF16), b.astype(BF16), dims)


def _split2(x):
    hi = x.astype(BF16)
    lo = (x - hi.astype(F32)).astype(BF16)
    return hi, lo


def _mm3(a, b, dims=_NN):
    ah, al = _split2(a)
    bh, bl = _split2(b)
    return _mm(ah, bh, dims) + (_mm(ah, bl, dims) + _mm(al, bh, dims))


def _mm_exact_lhs(a_bf, x):
    x1 = x.astype(BF16)
    r1 = x - x1.astype(F32)
    x2 = r1.astype(BF16)
    x3 = (r1 - x2.astype(F32)).astype(BF16)
    return _mm(a_bf, x1) + (_mm(a_bf, x2) + _mm(a_bf, x3))


def _seg_sum(x, ones_bd):
    hi, lo = _split2(x)
    return _mm(hi, ones_bd) + _mm(lo, ones_bd)


def _iota2(shape, dim):
    return lax.broadcasted_iota(jnp.int32, shape, dim)


def _block_ones(n, shift):
    r = jnp.right_shift(_iota2((n, n), 0), shift)
    c = jnp.right_shift(_iota2((n, n), 1), shift)
    return jnp.where(r == c, 1.0, 0.0).astype(BF16)


def _block_tril(n, shift):
    ri = _iota2((n, n), 0)
    ci = _iota2((n, n), 1)
    same = jnp.right_shift(ri, shift) == jnp.right_shift(ci, shift)
    return jnp.where(same & (ci <= ri), 1.0, 0.0).astype(BF16)


def _softplus(x):
    return jnp.maximum(x, 0.0) + jnp.log1p(jnp.exp(-jnp.abs(x)))


def _sigmoid(x):
    return 1.0 / (1.0 + jnp.exp(-x))


def _silu(x):
    return x * _sigmoid(x)


def _gelu_tanh(x):
    c = math.sqrt(2.0 / math.pi)
    return 0.5 * x * (1.0 + jnp.tanh(c * (x + 0.044715 * (x * x * x))))


def _rms(x, g):
    return x * lax.rsqrt(jnp.mean(x * x, axis=-1, keepdims=True) + EPS) * g


def _tri_masks():
    ri = _iota2((CHUNK, CHUNK), 0)
    ci = _iota2((CHUNK, CHUNK), 1)
    strict = ci < ri
    base = strict & (jnp.right_shift(ri, 3) == jnp.right_shift(ci, 3))
    levels = []
    for sh in (3, 4, 5):
        rb = jnp.right_shift(ri, sh)
        cb = jnp.right_shift(ci, sh)
        levels.append((rb == cb + 1) & ((rb & 1) == 1))
    eye = jnp.where(ri == ci, 1.0, 0.0).astype(F32)
    return strict, ci <= ri, base, levels, eye


def _inv_unit_lower(lmat, base, levels, eye):
    ld = jnp.where(base, lmat, 0.0)
    l2 = _mm3(ld, ld)
    l4 = _mm3(l2, l2)
    t = eye - ld
    t = t + _mm3(t, l2)
    t = t + _mm3(t, l4)
    for m in levels:
        off = jnp.where(m, lmat, 0.0)
        t = t - _mm3(_mm3(t, off), t)
    return t


def _causal_conv(xbuf, cur, w_ref, tc):
    xbuf[CONV_TAIL:CONV_TAIL + tc, :] = cur
    acc = None
    for i in range(CONV_W):
        off = CONV_TAIL - (CONV_W - 1) + i
        term = w_ref[i:i + 1, :] * xbuf[off:off + tc, :]
        acc = term if acc is None else acc + term
    xbuf[0:CONV_TAIL, :] = xbuf[tc:tc + CONV_TAIL, :]
    return acc


def _cparams(sem):
    return pltpu.CompilerParams(dimension_semantics=sem, vmem_limit_bytes=VMEM_LIMIT)


def _const_spec(shape):
    nd = len(shape)
    return pl.BlockSpec(shape, lambda *_: (0,) * nd)


def _inproj_kernel(x_ref, g_ref, w_ref, gdn_ref, lru_ref, s5_ref, rwkv_ref, ba_ref):
    h = _rms(x_ref[...], g_ref[...]).astype(BF16)
    p = _mm(h, w_ref[...])
    o = 0
    for ref in (gdn_ref, lru_ref, s5_ref, rwkv_ref, ba_ref):
        wdt = ref.shape[-1]
        ref[...] = p[:, o:o + wdt]
        o += wdt


def _in_proj(x2, norm_g, w_mix):
    t = x2.shape[0]
    tm = min(ROW_TILE, t)
    widths = (4 * MIX, LRU_COLS, S5_COLS, RWKV_COLS, LANE)
    return pl.pallas_call(
        _inproj_kernel,
        grid=(t // tm,),
        in_specs=[pl.BlockSpec((tm, D_MODEL), lambda i: (i, 0)),
                  _const_spec((1, D_MODEL)),
                  _const_spec((D_MODEL, MIX_COLS))],
        out_specs=[pl.BlockSpec((tm, w), lambda i: (i, 0)) for w in widths],
        out_shape=[jax.ShapeDtypeStruct((t, w), F32) for w in widths],
        compiler_params=_cparams(("parallel",)),
        name="in_proj",
    )(x2, norm_g.reshape(1, D_MODEL), w_mix)


def _gdn_kernel(p_ref, ba_ref, cw_ref, alog_ref, dtb_ref, ng_ref, y_ref, xbuf, s_scr, o_scr, *, tc):
    @pl.when(pl.program_id(1) == 0)
    def _():
        xbuf[0:CONV_TAIL, :] = jnp.zeros((CONV_TAIL, 3 * MIX), F32)
        s_scr[...] = jnp.zeros_like(s_scr)

    ones_bd = _block_ones(MIX, 6)
    strict, causal, base, levels, eye = _tri_masks()

    qkv = _silu(_causal_conv(xbuf, p_ref[:, 0:3 * MIX], cw_ref, tc))
    q = qkv[:, 0:MIX]
    k = qkv[:, MIX:2 * MIX]
    v = qkv[:, 2 * MIX:3 * MIX]
    q = q * lax.rsqrt(_seg_sum(q * q, ones_bd) + EPS) * (HEAD_DIM ** -0.5)
    k = k * lax.rsqrt(_seg_sum(k * k, ones_bd) + EPS)

    ba = ba_ref[...]
    beta = _sigmoid(ba)
    g = -jnp.exp(alog_ref[...]) * _softplus(ba + dtb_ref[...])
    gc = _mm_exact_lhs(_block_tril(tc, 6), g)
    gct = gc.T

    for c in range(tc // CHUNK):
        r0 = c * CHUNK
        for h in range(HEADS):
            l0 = h * HEAD_DIM
            gcol = gc[r0:r0 + CHUNK, HEADS + h:HEADS + h + 1]
            grow = gct[HEADS + h:HEADS + h + 1, r0:r0 + CHUNK]
            bcol = beta[r0:r0 + CHUNK, h:h + 1]
            qh = q[r0:r0 + CHUNK, l0:l0 + HEAD_DIM]
            kh = k[r0:r0 + CHUNK, l0:l0 + HEAD_DIM]
            vh = v[r0:r0 + CHUNK, l0:l0 + HEAD_DIM]
            dmat = jnp.where(causal, jnp.exp(jnp.where(causal, gcol - grow, 0.0)), 0.0)
            kb = kh * bcol
            lmat = jnp.where(strict, _mm3(kb, kh, _NT) * dmat, 0.0)
            tinv = _inv_unit_lower(lmat, base, levels, eye)
            eg = jnp.exp(gcol)
            u = _mm3(tinv, vh * bcol)
            w = _mm3(tinv, kb * eg)
            attn = _mm3(qh, kh, _NT) * dmat
            glast = gcol[CHUNK - 1:CHUNK, :]
            kdec = kh * jnp.exp(glast - gcol)
            state = s_scr[h]
            vnew = u - _mm3(w, state)
            o = _mm3(qh * eg, state) + _mm3(attn, vnew)
            s_scr[h] = state * jnp.exp(glast) + _mm3(kdec, vnew, _TN)
            o_scr[r0:r0 + CHUNK, l0:l0 + HEAD_DIM] = o

    o = o_scr[...]
    o = o * lax.rsqrt(_seg_sum(o * o, ones_bd) * (1.0 / HEAD_DIM) + EPS) * ng_ref[...]
    y_ref[...] = (o * _silu(p_ref[:, 3 * MIX:4 * MIX])).astype(y_ref.dtype)


def _gdn(p, ba, conv_w, a_log, dt_bias, norm_g):
    b, s, _ = p.shape
    tc = min(TC_GDN, s)
    pad = jnp.zeros((HEADS,), F32)
    lane_pad = jnp.zeros((LANE - 2 * HEADS,), F32)
    alog_row = jnp.concatenate([pad, a_log, lane_pad]).reshape(1, LANE)
    dtb_row = jnp.concatenate([pad, dt_bias, lane_pad]).reshape(1, LANE)
    ng_row = jnp.tile(norm_g, HEADS).reshape(1, MIX)
    return pl.pallas_call(
        functools.partial(_gdn_kernel, tc=tc),
        grid=(b, s // tc),
        in_specs=[pl.BlockSpec((None, tc, 4 * MIX), lambda i, j: (i, j, 0)),
                  pl.BlockSpec((None, tc, LANE), lambda i, j: (i, j, 0)),
                  _const_spec((CONV_W, 3 * MIX)),
                  _const_spec((1, LANE)), _const_spec((1, LANE)), _const_spec((1, MIX))],
        out_specs=pl.BlockSpec((None, tc, MIX), lambda i, j: (i, j, 0)),
        out_shape=jax.ShapeDtypeStruct((b, s, MIX), BF16),
        scratch_shapes=[pltpu.VMEM((tc + CONV_TAIL, 3 * MIX), F32),
                        pltpu.VMEM((HEADS, HEAD_DIM, HEAD_DIM), F32),
                        pltpu.VMEM((tc, MIX), F32)],
        compiler_params=_cparams(("parallel", "arbitrary")),
        name="gdn_mixer",
    )(p, ba, conv_w, alog_row, dtb_row, ng_row)


def _lru_kernel(p_ref, cw_ref, cb_ref, wa_ref, ba_ref, wx_ref, bx_ref, lam_ref, y_ref, xbuf, hcar, *, tc):
    @pl.when(pl.program_id(1) == 0)
    def _():
        xbuf[0:CONV_TAIL, :] = jnp.zeros((CONV_TAIL, MIX), F32)
        hcar[...] = jnp.zeros_like(hcar)

    u = _causal_conv(xbuf, p_ref[:, 0:MIX], cw_ref, tc) + cb_ref[...]
    ub = u.astype(BF16)
    r = _sigmoid(_mm(ub, wa_ref[...]) + ba_ref[...])
    i = _sigmoid(_mm(ub, wx_ref[...]) + bx_ref[...])
    log_a = (-LRU_C) * r * _softplus(-lam_ref[...])
    a = jnp.exp(log_a)
    inp = jnp.sqrt(-jnp.tanh(log_a) * (a * a + 1.0)) * (i * u)

    row = _iota2((tc, MIX), 0)
    acc_a, acc_b = a, inp
    sh = 1
    while sh < tc:
        keep = row >= sh
        a_sh = jnp.where(keep, pltpu.roll(acc_a, sh, 0), 1.0)
        b_sh = jnp.where(keep, pltpu.roll(acc_b, sh, 0), 0.0)
        acc_b = acc_b + acc_a * b_sh
        acc_a = acc_a * a_sh
        sh *= 2
    h = acc_b + acc_a * hcar[...]
    hcar[...] = h[tc - 1:tc, :]
    y_ref[...] = (h * _gelu_tanh(p_ref[:, MIX:2 * MIX])).astype(y_ref.dtype)


def _block_diag(w):
    n, d, e = w.shape
    eye = jnp.eye(n, dtype=w.dtype)
    return (eye[:, None, :, None] * w[:, :, None, :]).reshape(n * d, n * e)


def _lru(p, conv_w, conv_b, w_a, b_a, w_x, b_x, lam):
    b, s, _ = p.shape
    tc = min(TC_LRU, s)
    row = lambda t: t.reshape(1, MIX)
    return pl.pallas_call(
        functools.partial(_lru_kernel, tc=tc),
        grid=(b, s // tc),
        in_specs=[pl.BlockSpec((None, tc, LRU_COLS), lambda i, j: (i, j, 0)),
                  _const_spec((CONV_W, MIX)), _const_spec((1, MIX)),
                  _const_spec((MIX, MIX)), _const_spec((1, MIX)),
                  _const_spec((MIX, MIX)), _const_spec((1, MIX)), _const_spec((1, MIX))],
        out_specs=pl.BlockSpec((None, tc, MIX), lambda i, j: (i, j, 0)),
        out_shape=jax.ShapeDtypeStruct((b, s, MIX), BF16),
        scratch_shapes=[pltpu.VMEM((tc + CONV_TAIL, MIX), F32), pltpu.VMEM((1, MIX), F32)],
        compiler_params=_cparams(("parallel", "arbitrary")),
        name="rglru_mixer",
    )(p, conv_w, row(conv_b), _block_diag(w_a).astype(BF16), row(b_a),
      _block_diag(w_x).astype(BF16), row(b_x), row(lam))


def _s5_kernel(u_ref, bmat_ref, lstep_ref, lpow_ref, cmat_ref, d_ref, gw_ref, gb_ref, y_ref, hcar, *, tc):
    @pl.when(pl.program_id(1) == 0)
    def _():
        hcar[...] = jnp.zeros_like(hcar)

    u = u_ref[...]
    bu = _mm3(u, bmat_ref[...])
    hr = bu[:, 0:S5_LANES]
    hi = bu[:, S5_LANES:2 * S5_LANES]

    row = _iota2((tc, S5_LANES), 0)
    sh, kk = 1, 0
    while sh < tc:
        keep = row >= sh
        sr = jnp.where(keep, pltpu.roll(hr, sh, 0), 0.0)
        si = jnp.where(keep, pltpu.roll(hi, sh, 0), 0.0)
        lr = lstep_ref[kk:kk + 1, 0:S5_LANES]
        li = lstep_ref[kk:kk + 1, S5_LANES:2 * S5_LANES]
        hr, hi = hr + (lr * sr - li * si), hi + (lr * si + li * sr)
        sh *= 2
        kk += 1
    cr = hcar[:, 0:S5_LANES]
    ci = hcar[:, S5_LANES:2 * S5_LANES]
    pr = lpow_ref[:, 0:S5_LANES]
    pi = lpow_ref[:, S5_LANES:2 * S5_LANES]
    hr, hi = hr + (pr * cr - pi * ci), hi + (pr * ci + pi * cr)
    hcar[:, 0:S5_LANES] = hr[tc - 1:tc, :]
    hcar[:, S5_LANES:2 * S5_LANES] = hi[tc - 1:tc, :]

    y = _mm3(hr, cmat_ref[0:S5_LANES, :]) + _mm3(hi, cmat_ref[S5_LANES:2 * S5_LANES, :]) + d_ref[...] * u
    y = _gelu_tanh(y)
    y = y * _sigmoid(_mm1(y, gw_ref[...]) + gb_ref[...])
    y_ref[...] = y.astype(y_ref.dtype)


def _s5_params(lam_re, lam_im, b_re, b_im, c_re, c_im, log_dt, tc):
    dt = jnp.exp(log_dt)[:, None]
    mag = jnp.exp(lam_re * dt)
    lbr = mag * jnp.cos(lam_im * dt)
    lbi = mag * jnp.sin(lam_im * dt)
    den = lam_re * lam_re + lam_im * lam_im
    fr = ((lbr - 1.0) * lam_re + lbi * lam_im) / den
    fi = (lbi * lam_re - (lbr - 1.0) * lam_im) / den
    bbr = fr[..., None] * b_re - fi[..., None] * b_im
    bbi = fr[..., None] * b_im + fi[..., None] * b_re
    eye = jnp.eye(S5_GROUPS, dtype=F32)

    def in_blocks(t):
        t = jnp.transpose(t, (0, 2, 1))
        return (eye[:, None, :, None] * t[:, :, None, :]).reshape(MIX, S5_LANES)

    def out_blocks(t):
        t = jnp.transpose(t, (0, 2, 1))
        return (eye[:, None, :, None] * t[:, :, None, :]).reshape(S5_LANES, MIX)

    bmat = jnp.concatenate([in_blocks(bbr), in_blocks(bbi)], axis=1)
    cmat = jnp.concatenate([out_blocks(c_re), out_blocks(-c_im)], axis=0)

    pr = lbr.reshape(1, S5_LANES)
    pi = lbi.reshape(1, S5_LANES)
    steps = []
    n = 1
    while n < tc:
        fr_, fi_ = pr[n - 1:n], pi[n - 1:n]
        steps.append(jnp.concatenate([fr_, fi_], axis=1))
        pr, pi = (jnp.concatenate([pr, pr * fr_ - pi * fi_], axis=0),
                  jnp.concatenate([pi, pr * fi_ + pi * fr_], axis=0))
        n *= 2
    lstep = jnp.concatenate(steps, axis=0)
    lpow = jnp.concatenate([pr, pi], axis=1)
    return bmat, cmat, lstep, lpow


def _s5(u, lam_re, lam_im, b_re, b_im, c_re, c_im, d, log_dt, glu_w, glu_b):
    b, s, _ = u.shape
    tc = min(TC_S5, s)
    bmat, cmat, lstep, lpow = _s5_params(lam_re, lam_im, b_re, b_im, c_re, c_im, log_dt, tc)
    nstep = lstep.shape[0]
    return pl.pallas_call(
        functools.partial(_s5_kernel, tc=tc),
        grid=(b, s // tc),
        in_specs=[pl.BlockSpec((None, tc, MIX), lambda i, j: (i, j, 0)),
                  _const_spec((MIX, 2 * S5_LANES)),
                  _const_spec((nstep, 2 * S5_LANES)),
                  _const_spec((tc, 2 * S5_LANES)),
                  _const_spec((2 * S5_LANES, MIX)),
                  _const_spec((1, MIX)), _const_spec((MIX, MIX)), _const_spec((1, MIX))],
        out_specs=pl.BlockSpec((None, tc, MIX), lambda i, j: (i, j, 0)),
        out_shape=jax.ShapeDtypeStruct((b, s, MIX), BF16),
        scratch_shapes=[pltpu.VMEM((1, 2 * S5_LANES), F32)],
        compiler_params=_cparams(("parallel", "arbitrary")),
        name="s5_mixer",
    )(u, bmat, lstep, lpow, cmat, d.reshape(1, MIX), glu_w.astype(BF16), glu_b.reshape(1, MIX))


def _rwkv_kernel(p_ref, mu_ref, w0_ref, wup_ref, a0_ref, aup_ref, gup_ref, kk_ref, ka_ref, rk_ref,
                 lng_ref, lnb_ref, y_ref, prev, s_scr, o_scr, *, tc):
    @pl.when(pl.program_id(1) == 0)
    def _():
        prev[...] = jnp.zeros_like(prev)
        s_scr[...] = jnp.zeros_like(s_scr)

    ones_bd = _block_ones(MIX, 6)
    strict, causal, base, levels, eye = _tri_masks()

    p = p_ref[...]
    row = _iota2((tc, RWKV_COLS), 0)
    shifted = jnp.where(row == 0, prev[...], pltpu.roll(p, 1, 0))
    prev[...] = p[tc - 1:tc, :]
    p = p + mu_ref[...] * (shifted - p)
    r = p[:, 0:MIX]
    k = p[:, MIX:2 * MIX]
    v = p[:, 2 * MIX:3 * MIX]
    o0 = 3 * MIX
    wd = p[:, o0:o0 + DECAY_LORA]
    ad = p[:, o0 + DECAY_LORA:o0 + DECAY_LORA + AAA_LORA]
    gd = p[:, o0 + DECAY_LORA + AAA_LORA:RWKV_COLS]

    logw = -_softplus(-(w0_ref[...] + _mm1(jnp.tanh(wd), wup_ref[...]))) - 0.5
    ld = -jnp.exp(logw)
    a = _sigmoid(a0_ref[...] + _mm1(ad, aup_ref[...]))
    g = _mm1(_sigmoid(gd), gup_ref[...])
    kk = k * kk_ref[...]
    kk = kk * lax.rsqrt(_seg_sum(kk * kk, ones_bd) + EPS)
    k = k * (1.0 + (a - 1.0) * ka_ref[...])

    cum = _mm_exact_lhs(_block_tril(tc, 6), ld)
    ecum = jnp.exp(cum)
    ncum = jnp.exp(-cum)
    r_t = r * ecum
    b_t = kk * jnp.exp(cum - ld)
    k_t = k * ncum
    a_t = (a * kk) * ncum

    for c in range(tc // CHUNK):
        r0 = c * CHUNK
        clast = cum[r0 + CHUNK - 1:r0 + CHUNK, :]
        tail = jnp.exp(clast - cum[r0:r0 + CHUNK, :])
        kend = k[r0:r0 + CHUNK, :] * tail
        aend = (a * kk)[r0:r0 + CHUNK, :] * tail
        pl_row = jnp.exp(clast)
        for h in range(HEADS):
            l0 = h * HEAD_DIM
            sl = (slice(r0, r0 + CHUNK), slice(l0, l0 + HEAD_DIM))
            rh, bh, kh, ah, vh = r_t[sl], b_t[sl], k_t[sl], a_t[sl], v[sl]
            a_ba = jnp.where(strict, _mm3(bh, ah, _NT), 0.0)
            a_bk = jnp.where(strict, _mm3(bh, kh, _NT), 0.0)
            a_rk = jnp.where(causal, _mm3(rh, kh, _NT), 0.0)
            a_ra = jnp.where(causal, _mm3(rh, ah, _NT), 0.0)
            tinv = _inv_unit_lower(a_ba, base, levels, eye)
            state = s_scr[h]
            z = _mm3(tinv, _mm3(bh, state, _NT) + _mm3(a_bk, vh))
            o = _mm3(rh, state, _NT) + _mm3(a_rk, vh) - _mm3(a_ra, z)
            s_scr[h] = (state * pl_row[:, l0:l0 + HEAD_DIM]
                        + _mm3(vh, kend[:, l0:l0 + HEAD_DIM], _TN)
                        - _mm3(z, aend[:, l0:l0 + HEAD_DIM], _TN))
            o_scr[r0:r0 + CHUNK, l0:l0 + HEAD_DIM] = o

    o = o_scr[...]
    inv_n = 1.0 / HEAD_DIM
    mean = _seg_sum(o, ones_bd) * inv_n
    cen = o - mean
    var = _seg_sum(cen * cen, ones_bd) * inv_n
    o = cen * lax.rsqrt(var + RWKV_LN_EPS) * lng_ref[...] + lnb_ref[...]
    bonus = _seg_sum(r * k * rk_ref[...], ones_bd) * v
    y_ref[...] = ((o + bonus) * g).astype(y_ref.dtype)


def _rwkv(p, mu, w0, w_up, a0, a_up, g_up, k_k, k_a, r_k, ln_g, ln_b):
    b, s, _ = p.shape
    tc = min(TC_RWKV, s)
    row = lambda t: t.reshape(1, MIX)
    return pl.pallas_call(
        functools.partial(_rwkv_kernel, tc=tc),
        grid=(b, s // tc),
        in_specs=[pl.BlockSpec((None, tc, RWKV_COLS), lambda i, j: (i, j, 0)),
                  _const_spec((1, RWKV_COLS)),
                  _const_spec((1, MIX)), _const_spec((DECAY_LORA, MIX)),
                  _const_spec((1, MIX)), _const_spec((AAA_LORA, MIX)),
                  _const_spec((GATE_LORA, MIX)),
                  _const_spec((1, MIX)), _const_spec((1, MIX)), _const_spec((1, MIX)),
                  _const_spec((1, MIX)), _const_spec((1, MIX))],
        out_specs=pl.BlockSpec((None, tc, MIX), lambda i, j: (i, j, 0)),
        out_shape=jax.ShapeDtypeStruct((b, s, MIX), BF16),
        scratch_shapes=[pltpu.VMEM((1, RWKV_COLS), F32),
                        pltpu.VMEM((HEADS, HEAD_DIM, HEAD_DIM), F32),
                        pltpu.VMEM((tc, MIX), F32)],
        compiler_params=_cparams(("parallel", "arbitrary")),
        name="rwkv7_mixer",
    )(p, mu.reshape(1, RWKV_COLS), row(w0), w_up.astype(BF16), row(a0), a_up.astype(BF16),
      g_up.astype(BF16), row(k_k), row(k_a), row(r_k.reshape(MIX)), row(ln_g), row(ln_b))


def _merge_kernel(x_ref, g_ref, wg_ref, y0_ref, y1_ref, y2_ref, y3_ref, wb_ref, wo_ref, o_ref):
    x = x_ref[...]
    h = _rms(x, g_ref[...]).astype(BF16)
    merged = None
    for i, y_ref in enumerate((y0_ref, y1_ref, y2_ref, y3_ref)):
        gate = _sigmoid(_mm(h, wg_ref[:, i * D_MODEL:(i + 1) * D_MODEL]))
        term = gate * _mm(y_ref[...], wb_ref[i])
        merged = term if merged is None else merged + term
    o_ref[...] = x + _mm(merged.astype(BF16), wo_ref[...])


def _merge(x2, norm_g, w_gate, ys, w_branch, w_out):
    t = x2.shape[0]
    tm = min(ROW_TILE, t)
    return pl.pallas_call(
        _merge_kernel,
        grid=(t // tm,),
        in_specs=[pl.BlockSpec((tm, D_MODEL), lambda i: (i, 0)),
                  _const_spec((1, D_MODEL)),
                  _const_spec((D_MODEL, HEADS * D_MODEL))]
                 + [pl.BlockSpec((tm, MIX), lambda i: (i, 0)) for _ in range(4)]
                 + [_const_spec((4, MIX, D_MODEL)), _const_spec((D_MODEL, D_MODEL))],
        out_specs=pl.BlockSpec((tm, D_MODEL), lambda i: (i, 0)),
        out_shape=jax.ShapeDtypeStruct((t, D_MODEL), F32),
        compiler_params=_cparams(("parallel",)),
        name="gated_merge",
    )(x2, norm_g.reshape(1, D_MODEL), w_gate, *ys, w_branch, w_out)


def _mlp_kernel(x_ref, g_ref, w1_ref, w2_ref, gf_ref, o_ref, *, final_norm):
    x = x_ref[...]
    h = _rms(x, g_ref[...]).astype(BF16)
    a = jnp.maximum(_mm(h, w1_ref[...]), 0.0)
    x = x + _mm((a * a).astype(BF16), w2_ref[...])
    if final_norm:
        x = _rms(x, gf_ref[...])
    o_ref[...] = x


def _mlp(x2, norm_g, w1, w2, final_g, final_norm):
    t = x2.shape[0]
    tm = min(ROW_TILE, t)
    return pl.pallas_call(
        functools.partial(_mlp_kernel, final_norm=final_norm),
        grid=(t // tm,),
        in_specs=[pl.BlockSpec((tm, D_MODEL), lambda i: (i, 0)),
                  _const_spec((1, D_MODEL)),
                  _const_spec((D_MODEL, D_FF)), _const_spec((D_FF, D_MODEL)),
                  _const_spec((1, D_MODEL))],
        out_specs=pl.BlockSpec((tm, D_MODEL), lambda i: (i, 0)),
        out_shape=jax.ShapeDtypeStruct((t, D_MODEL), F32),
        compiler_params=_cparams(("parallel",)),
        name="mlp",
    )(x2, norm_g.reshape(1, D_MODEL), w1, w2, final_g.reshape(1, D_MODEL))


def _mix_weight(w_in):
    qkvz = w_in[:, 0:4 * MIX]
    ba = w_in[:, 4 * MIX:GDN_COLS]
    rest = w_in[:, GDN_COLS:GDN_COLS + LRU_COLS + S5_COLS + RWKV_COLS]
    pad = jnp.zeros((D_MODEL, LANE - 2 * HEADS), w_in.dtype)
    return jnp.concatenate([qkvz, rest, ba, pad], axis=1).astype(BF16)


def kernel(x, norm1_g, w_in, gdn_conv_w, gdn_a_log, gdn_dt_bias, gdn_norm_g, lru_conv_w, lru_conv_b, lru_w_a, lru_b_a, lru_w_x, lru_b_x, lru_lambda, s5_lambda_re, s5_lambda_im, s5_b_re, s5_b_im, s5_c_re, s5_c_im, s5_d, s5_log_dt, s5_glu_w, s5_glu_b, rwkv_mu, rwkv_w0, rwkv_w_up, rwkv_a0, rwkv_a_up, rwkv_g_up, rwkv_k_k, rwkv_k_a, rwkv_r_k, rwkv_ln_g, rwkv_ln_b, w_branch, w_out, norm2_g, mlp_w1, mlp_w2, final_norm_g):
    b, s, d = x.shape
    depth = w_in.shape[0]
    x2 = x.reshape(b * s, d)
    gate_off = GDN_COLS + LRU_COLS + S5_COLS + RWKV_COLS
    for l in range(depth):
        p_gdn, p_lru, p_s5, p_rwkv, p_ba = _in_proj(x2, norm1_g[l], _mix_weight(w_in[l]))
        shp = lambda t: t.reshape(b, s, t.shape[-1])
        ys = (
            _gdn(shp(p_gdn), shp(p_ba), gdn_conv_w[l], gdn_a_log[l], gdn_dt_bias[l], gdn_norm_g[l]),
            _lru(shp(p_lru), lru_conv_w[l], lru_conv_b[l], lru_w_a[l], lru_b_a[l], lru_w_x[l], lru_b_x[l],
                 lru_lambda[l]),
            _s5(shp(p_s5), s5_lambda_re[l], s5_lambda_im[l], s5_b_re[l], s5_b_im[l], s5_c_re[l], s5_c_im[l],
                s5_d[l], s5_log_dt[l], s5_glu_w[l], s5_glu_b[l]),
            _rwkv(shp(p_rwkv), rwkv_mu[l], rwkv_w0[l], rwkv_w_up[l], rwkv_a0[l], rwkv_a_up[l], rwkv_g_up[l],
                  rwkv_k_k[l], rwkv_k_a[l], rwkv_r_k[l], rwkv_ln_g[l], rwkv_ln_b[l]),
        )
        ys = tuple(y.reshape(b * s, MIX) for y in ys)
        x2 = _merge(x2, norm1_g[l], w_in[l][:, gate_off:].astype(BF16), ys,
                    w_branch[l].astype(BF16), w_out[l].astype(BF16))
        x2 = _mlp(x2, norm2_g[l], mlp_w1[l].astype(BF16), mlp_w2[l].astype(BF16), final_norm_g,
                  final_norm=(l == depth - 1))
    return x2.reshape(b, s, d)
```

```python
import functools
import math

import jax
import jax.numpy as jnp
from jax import lax
from jax.experimental import pallas as pl
from jax.experimental.pallas import tpu as pltpu

F32 = jnp.float32
BF16 = jnp.bfloat16

D_MODEL = 1024
MIX = 256
HEADS = 4
HEAD_DIM = 64
CHUNK = 64
CONV_W = 4
EPS = 1e-6
LRU_C = 8.0
S5_GROUPS = 16
S5_GROUP = 16
S5_STATE = 64
S5_LANES = S5_GROUPS * S5_STATE
DECAY_LORA = 64
AAA_LORA = 64
GATE_LORA = 128
RWKV_LN_EPS = 64e-5
D_FF = 4 * D_MODEL
LANE = 128
CONV_TAIL = 8

GDN_COLS = 4 * MIX + 2 * HEADS
LRU_COLS = 2 * MIX
S5_COLS = MIX
RWKV_COLS = 3 * MIX + DECAY_LORA + AAA_LORA + GATE_LORA
GDN_GATE_COLS = 2 * MIX
MIX_COLS = 4 * MIX + LRU_COLS + S5_COLS + RWKV_COLS + GDN_GATE_COLS

ROW_TILE = 512
TC_GDN = 256
TC_RWKV = 256
TC_LRU = 256
S5_BLOCK = 16
S5_NBLK = 256
VMEM_LIMIT = 56 * 1024 * 1024

_P_QK = 1
_P_A = 1
_P_INV = 1
_P_UW = 1
_P_STATE = 1

_NN = (((1,), (0,)), ((), ()))
_NT = (((1,), (1,)), ((), ()))
_TN = (((0,), (0,)), ((), ()))


def _mm(a, b, dims=_NN):
    return lax.dot_general(a, b, dims, preferred_element_type=F32)


def _mm1(a, b, dims=_NN):
    return _mm(a.astype(BF16), b.astype(BF16), dims)


def _split2(x):
    hi = x.astype(BF16)
    lo = (x - hi.astype(F32)).astype(BF16)
    return hi, lo


def _mm_exact_lhs(a_bf, x):
    x1 = x.astype(BF16)
    r1 = x - x1.astype(F32)
    x2 = r1.astype(BF16)
    x3 = (r1 - x2.astype(F32)).astype(BF16)
    return _mm(a_bf, x1) + (_mm(a_bf, x2) + _mm(a_bf, x3))


def _seg_sum(x, ones_bd):
    hi, lo = _split2(x)
    return _mm(hi, ones_bd) + _mm(lo, ones_bd)


def _iota2(shape, dim):
    return lax.broadcasted_iota(jnp.int32, shape, dim)


def _block_ones(n, shift):
    r = jnp.right_shift(_iota2((n, n), 0), shift)
    c = jnp.right_shift(_iota2((n, n), 1), shift)
    return jnp.where(r == c, 1.0, 0.0).astype(BF16)


def _block_tril(n, shift):
    ri = _iota2((n, n), 0)
    ci = _iota2((n, n), 1)
    same = jnp.right_shift(ri, shift) == jnp.right_shift(ci, shift)
    return jnp.where(same & (ci <= ri), 1.0, 0.0).astype(BF16)


def _softplus(x):
    return jnp.maximum(x, 0.0) + jnp.log1p(jnp.exp(-jnp.abs(x)))


def _sigmoid(x):
    return 1.0 / (1.0 + jnp.exp(-x))


def _silu(x):
    return x * _sigmoid(x)


def _gelu_tanh(x):
    c = math.sqrt(2.0 / math.pi)
    return 0.5 * x * (1.0 + jnp.tanh(c * (x + 0.044715 * (x * x * x))))


def _rms(x, g):
    return x * lax.rsqrt(jnp.mean(x * x, axis=-1, keepdims=True) + EPS) * g


PAIR = 2 * HEAD_DIM
PAIRS = (slice(0, PAIR), slice(PAIR, 2 * PAIR))


class _HeadMasks:
    def __init__(self):
        ri = _iota2((CHUNK, MIX), 0)
        ci = _iota2((CHUNK, MIX), 1) & (HEAD_DIM - 1)
        self.strict = ci < ri
        self.causal = ci <= ri
        self.upper = ri <= ci
        self.base = self.strict & (jnp.right_shift(ri, 3) == jnp.right_shift(ci, 3))
        self.levels = []
        for sh in (3, 4, 5):
            rb = jnp.right_shift(ri, sh)
            cb = jnp.right_shift(ci, sh)
            self.levels.append((rb == cb + 1) & ((rb & 1) == 1))
        self.eye = jnp.where(ri == ci, 1.0, 0.0).astype(F32)
        first = _iota2((CHUNK, PAIR), 1) < HEAD_DIM
        self.lane_lo = jnp.where(first, 1.0, 0.0).astype(BF16)
        self.lane_hi = jnp.where(first, 0.0, 1.0).astype(BF16)
        r2 = jnp.right_shift(_iota2((PAIR, PAIR), 0), 6)
        c2 = jnp.right_shift(_iota2((PAIR, PAIR), 1), 6)
        self.bd = r2 == c2
        self.ones_bd = _block_ones(MIX, 6)


def _embed(xb, hm):
    return jnp.concatenate([xb * hm.lane_lo, xb * hm.lane_hi], axis=0)


def _hmm(a, b, hm, dims=_NN, passes=3):
    m = a.shape[0]
    outs = []
    for s in PAIRS:
        if passes == 1:
            outs.append(_mm(a[:, s].astype(BF16), _embed(b[:, s].astype(BF16), hm), dims))
        else:
            ah, al = _split2(a[:, s])
            bh, bl = _split2(b[:, s])
            top = _mm(jnp.concatenate([ah, al], axis=0), _embed(bh, hm), dims)
            outs.append(top[:m] + (top[m:] + _mm(ah, _embed(bl, hm), dims)))
    return jnp.concatenate(outs, axis=1)


def _inv_unit_lower(lmats, hm):
    p = _P_INV
    ld = [jnp.where(hm.base, l, 0.0) for l in lmats]
    l2 = [_hmm(x, x, hm, _NN, p) for x in ld]
    l4 = [_hmm(x, x, hm, _NN, p) for x in l2]
    t = [hm.eye - x for x in ld]
    t = [ti + _hmm(ti, x, hm, _NN, p) for ti, x in zip(t, l2)]
    t = [ti + _hmm(ti, x, hm, _NN, p) for ti, x in zip(t, l4)]
    for m in hm.levels:
        off = [jnp.where(m, l, 0.0) for l in lmats]
        ta = [_hmm(ti, x, hm, _NN, p) for ti, x in zip(t, off)]
        t = [ti - _hmm(x, ti, hm, _NN, p) for ti, x in zip(t, ta)]
    return t


def _state_mm(lhs, state, dims=_NN):
    outs = []
    for s, st in zip(PAIRS, state):
        lh = lhs[:, s].astype(BF16)
        sh = st.astype(BF16)
        out = _mm(lh, sh, dims)
        if _P_STATE >= 2:
            out = out + _mm(lh, (st - sh.astype(F32)).astype(BF16), dims)
        if _P_STATE >= 3:
            out = out + _mm((lhs[:, s] - lh.astype(F32)).astype(BF16), sh, dims)
        outs.append(out)
    return jnp.concatenate(outs, axis=1)


def _state_update(state, decay_row, lhs, rhs, hm):
    new = []
    for s, st in zip(PAIRS, state):
        upd = _mm(lhs[:, s].astype(BF16), rhs[:, s].astype(BF16), _TN)
        new.append(st * decay_row[:, s] + jnp.where(hm.bd, upd, 0.0))
    return new


def _causal_conv(xbuf, cur, w_ref, tc):
    xbuf[CONV_TAIL:CONV_TAIL + tc, :] = cur
    acc = None
    for i in range(CONV_W):
        off = CONV_TAIL - (CONV_W - 1) + i
        term = w_ref[i:i + 1, :] * xbuf[off:off + tc, :]
        acc = term if acc is None else acc + term
    xbuf[0:CONV_TAIL, :] = xbuf[tc:tc + CONV_TAIL, :]
    return acc


def _cparams(sem):
    return pltpu.CompilerParams(dimension_semantics=sem, vmem_limit_bytes=VMEM_LIMIT)


def _const_spec(shape):
    nd = len(shape)
    return pl.BlockSpec(shape, lambda *_: (0,) * nd)


def _inproj_kernel(x_ref, g_ref, w_ref, gdn_ref, lru_ref, s5_ref, rwkv_ref, ba_ref):
    h = _rms(x_ref[...], g_ref[...]).astype(BF16)
    p = _mm(h, w_ref[...])
    o = 0
    for ref in (gdn_ref, lru_ref, s5_ref, rwkv_ref, ba_ref):
        wdt = ref.shape[-1]
        ref[...] = p[:, o:o + wdt]
        o += wdt


def _in_proj(x2, norm_g, w_mix):
    t = x2.shape[0]
    tm = min(ROW_TILE, t)
    widths = (4 * MIX, LRU_COLS, S5_COLS, RWKV_COLS, GDN_GATE_COLS)
    return pl.pallas_call(
        _inproj_kernel,
        grid=(t // tm,),
        in_specs=[pl.BlockSpec((tm, D_MODEL), lambda i: (i, 0)),
                  _const_spec((1, D_MODEL)),
                  _const_spec((D_MODEL, MIX_COLS))],
        out_specs=[pl.BlockSpec((tm, w), lambda i: (i, 0)) for w in widths],
        out_shape=[jax.ShapeDtypeStruct((t, w), F32) for w in widths],
        compiler_params=_cparams(("parallel",)),
        name="in_proj",
    )(x2, norm_g.reshape(1, D_MODEL), w_mix)


def _gdn_kernel(p_ref, gl_ref, cw_ref, alog_ref, dtb_ref, ng_ref, y_ref, xbuf, s_scr, o_scr, *, tc, nb):
    @pl.when(pl.program_id(0) == 0)
    def _():
        xbuf[:, 0:CONV_TAIL, :] = jnp.zeros((nb, CONV_TAIL, 3 * MIX), F32)
        s_scr[...] = jnp.zeros_like(s_scr)

    hm = _HeadMasks()
    ones_bd = hm.ones_bd
    ones_cc = jnp.ones((CHUNK, CHUNK), BF16)
    tril = _block_tril(tc, 6)
    nc = tc // CHUNK

    items = []
    for bi in range(nb):
        qkv = _silu(_causal_conv(xbuf.at[bi], p_ref[bi, :, 0:3 * MIX], cw_ref, tc))
        q = qkv[:, 0:MIX]
        k = qkv[:, MIX:2 * MIX]
        v = qkv[:, 2 * MIX:3 * MIX]
        q = q * lax.rsqrt(_seg_sum(q * q, ones_bd) + EPS) * (HEAD_DIM ** -0.5)
        k = k * lax.rsqrt(_seg_sum(k * k, ones_bd) + EPS)
        beta = _sigmoid(gl_ref[bi, :, 0:MIX])
        g = -jnp.exp(alog_ref[...]) * _softplus(gl_ref[bi, :, MIX:2 * MIX] + dtb_ref[...])
        gc = _mm_exact_lhs(tril, g)
        for c in range(nc):
            rows = slice(c * CHUNK, (c + 1) * CHUNK)
            items.append(dict(bi=bi, c=c, q=q[rows], k=k[rows], v=v[rows], beta=beta[rows],
                              g=g[rows], gc=gc[rows]))

    for it in items:
        g2 = _mm_exact_lhs(ones_cc, jnp.where(hm.upper, it["g"], 0.0))
        diff = jnp.where(hm.causal, it["gc"] - g2, 0.0)
        it["dmat"] = jnp.where(hm.causal, jnp.exp(diff), 0.0)
        it["kb"] = it["k"] * it["beta"]
    for it in items:
        kk = _hmm(it["kb"], it["k"], hm, _NT, _P_QK)
        it["lmat"] = jnp.where(hm.strict, kk * it["dmat"], 0.0)
    tinvs = _inv_unit_lower([it["lmat"] for it in items], hm)
    for it, tinv in zip(items, tinvs):
        eg = jnp.exp(it["gc"])
        it["u"] = _hmm(tinv, it["v"] * it["beta"], hm, _NN, _P_UW)
        it["w"] = _hmm(tinv, it["kb"] * eg, hm, _NN, _P_UW)
        it["attn"] = _hmm(it["q"], it["k"], hm, _NT, _P_QK) * it["dmat"]
        it["qd"] = it["q"] * eg
        glast = it["gc"][CHUNK - 1:CHUNK, :]
        it["kdec"] = it["k"] * jnp.exp(glast - it["gc"])
        it["elast"] = jnp.exp(glast)

    states = [[s_scr[bi, pi] for pi in range(len(PAIRS))] for bi in range(nb)]
    for c in range(nc):
        for bi in range(nb):
            it = items[bi * nc + c]
            ws = _state_mm(jnp.concatenate([it["w"], it["qd"]], axis=0), states[bi])
            vnew = it["u"] - ws[:CHUNK]
            o = ws[CHUNK:] + _hmm(it["attn"], vnew, hm, _NN, 1)
            states[bi] = _state_update(states[bi], it["elast"], it["kdec"], vnew, hm)
            o_scr[bi, c * CHUNK:(c + 1) * CHUNK, :] = o
    for bi in range(nb):
        for pi in range(len(PAIRS)):
            s_scr[bi, pi] = states[bi][pi]

    for bi in range(nb):
        o = o_scr[bi]
        o = o * lax.rsqrt(_seg_sum(o * o, ones_bd) * (1.0 / HEAD_DIM) + EPS) * ng_ref[...]
        y_ref[bi] = (o * _silu(p_ref[bi, :, 3 * MIX:4 * MIX])).astype(y_ref.dtype)


def _head_row(t):
    return jnp.repeat(t, HEAD_DIM).reshape(1, MIX)


def _gdn(p, gl, conv_w, a_log, dt_bias, norm_g):
    b, s, _ = p.shape
    tc = min(TC_GDN, s)
    ng_row = jnp.tile(norm_g, HEADS).reshape(1, MIX)
    return pl.pallas_call(
        functools.partial(_gdn_kernel, tc=tc, nb=b),
        grid=(s // tc,),
        in_specs=[pl.BlockSpec((b, tc, 4 * MIX), lambda j: (0, j, 0)),
                  pl.BlockSpec((b, tc, GDN_GATE_COLS), lambda j: (0, j, 0)),
                  _const_spec((CONV_W, 3 * MIX)),
                  _const_spec((1, MIX)), _const_spec((1, MIX)), _const_spec((1, MIX))],
        out_specs=pl.BlockSpec((b, tc, MIX), lambda j: (0, j, 0)),
        out_shape=jax.ShapeDtypeStruct((b, s, MIX), BF16),
        scratch_shapes=[pltpu.VMEM((b, tc + CONV_TAIL, 3 * MIX), F32),
                        pltpu.VMEM((b, len(PAIRS), PAIR, PAIR), F32),
                        pltpu.VMEM((b, tc, MIX), F32)],
        compiler_params=_cparams(("arbitrary",)),
        name="gdn_mixer",
    )(p, gl, conv_w, _head_row(a_log), _head_row(dt_bias), ng_row)


def _lru_kernel(p_ref, cw_ref, cb_ref, wa_ref, ba_ref, wx_ref, bx_ref, lam_ref, y_ref, xbuf, hcar, *, tc):
    @pl.when(pl.program_id(1) == 0)
    def _():
        xbuf[0:CONV_TAIL, :] = jnp.zeros((CONV_TAIL, MIX), F32)
        hcar[...] = jnp.zeros_like(hcar)

    u = _causal_conv(xbuf, p_ref[:, 0:MIX], cw_ref, tc) + cb_ref[...]
    ub = u.astype(BF16)
    r = _sigmoid(_mm(ub, wa_ref[...]) + ba_ref[...])
    i = _sigmoid(_mm(ub, wx_ref[...]) + bx_ref[...])
    log_a = (-LRU_C) * r * _softplus(-lam_ref[...])
    a = jnp.exp(log_a)
    inp = jnp.sqrt(-jnp.tanh(log_a) * (a * a + 1.0)) * (i * u)

    row = _iota2((tc, MIX), 0)
    acc_a, acc_b = a, inp
    sh = 1
    while sh < tc:
        keep = row >= sh
        a_sh = jnp.where(keep, pltpu.roll(acc_a, sh, 0), 1.0)
        b_sh = jnp.where(keep, pltpu.roll(acc_b, sh, 0), 0.0)
        acc_b = acc_b + acc_a * b_sh
        acc_a = acc_a * a_sh
        sh *= 2
    h = acc_b + acc_a * hcar[...]
    hcar[...] = h[tc - 1:tc, :]
    y_ref[...] = (h * _gelu_tanh(p_ref[:, MIX:2 * MIX])).astype(y_ref.dtype)


def _block_diag(w):
    n, d, e = w.shape
    eye = jnp.eye(n, dtype=w.dtype)
    return (eye[:, None, :, None] * w[:, :, None, :]).reshape(n * d, n * e)


def _lru(p, conv_w, conv_b, w_a, b_a, w_x, b_x, lam):
    b, s, _ = p.shape
    tc = min(TC_LRU, s)
    row = lambda t: t.reshape(1, MIX)
    return pl.pallas_call(
        functools.partial(_lru_kernel, tc=tc),
        grid=(b, s // tc),
        in_specs=[pl.BlockSpec((None, tc, LRU_COLS), lambda i, j: (i, j, 0)),
                  _const_spec((CONV_W, MIX)), _const_spec((1, MIX)),
                  _const_spec((MIX, MIX)), _const_spec((1, MIX)),
                  _const_spec((MIX, MIX)), _const_spec((1, MIX)), _const_spec((1, MIX))],
        out_specs=pl.BlockSpec((None, tc, MIX), lambda i, j: (i, j, 0)),
        out_shape=jax.ShapeDtypeStruct((b, s, MIX), BF16),
        scratch_shapes=[pltpu.VMEM((tc + CONV_TAIL, MIX), F32), pltpu.VMEM((1, MIX), F32)],
        compiler_params=_cparams(("parallel", "arbitrary")),
        name="rglru_mixer",
    )(p, conv_w, row(conv_b), _block_diag(w_a).astype(BF16), row(b_a),
      _block_diag(w_x).astype(BF16), row(b_x), row(lam))


def _s5_ssm_kernel(u_ref, e_ref, m_ref, p_ref, lstep_ref, lpow_ref, y_ref, hcar, *, nblk):
    @pl.when(pl.program_id(1) == 0)
    def _():
        hcar[...] = jnp.zeros_like(hcar)

    ub = [u_ref[g].astype(BF16) for g in range(S5_GROUPS)]
    res, ims = [], []
    for q in range(S5_GROUPS // 2):
        e = _mm(jnp.concatenate([ub[2 * q], ub[2 * q + 1]], axis=1), e_ref[q])
        res.append(e[:, 0:LANE])
        ims.append(e[:, LANE:2 * LANE])
    hr = jnp.concatenate(res, axis=1)
    hi = jnp.concatenate(ims, axis=1)

    row = _iota2((nblk, S5_LANES), 0)
    sh, kk = 1, 0
    while sh < nblk:
        keep = row >= sh
        sr = jnp.where(keep, pltpu.roll(hr, sh, 0), 0.0)
        si = jnp.where(keep, pltpu.roll(hi, sh, 0), 0.0)
        lr = lstep_ref[kk:kk + 1, 0:S5_LANES]
        li = lstep_ref[kk:kk + 1, S5_LANES:2 * S5_LANES]
        hr, hi = hr + (lr * sr - li * si), hi + (lr * si + li * sr)
        sh *= 2
        kk += 1
    cr = hcar[:, 0:S5_LANES]
    ci = hcar[:, S5_LANES:2 * S5_LANES]
    pr = lpow_ref[:, 0:S5_LANES]
    pi = lpow_ref[:, S5_LANES:2 * S5_LANES]
    hr, hi = hr + (pr * cr - pi * ci), hi + (pr * ci + pi * cr)
    hcar[:, 0:S5_LANES] = hr[nblk - 1:nblk, :]
    hcar[:, S5_LANES:2 * S5_LANES] = hi[nblk - 1:nblk, :]
    first = row == 0
    hr = jnp.where(first, cr, pltpu.roll(hr, 1, 0))
    hi = jnp.where(first, ci, pltpu.roll(hi, 1, 0))

    for q in range(S5_GROUPS // 2):
        hp = jnp.concatenate([hr[:, q * LANE:(q + 1) * LANE], hi[:, q * LANE:(q + 1) * LANE]], axis=1)
        hp_hi, hp_lo = _split2(hp)
        for g in (2 * q, 2 * q + 1):
            pm = p_ref[g]
            y_ref[g] = _mm(ub[g], m_ref[g]) + (_mm(hp_hi, pm) + _mm(hp_lo, pm))


def _s5_out_kernel(s_ref, u_ref, d_ref, gw_ref, gb_ref, y_ref):
    y = _gelu_tanh(s_ref[...] + d_ref[...] * u_ref[...])
    y = y * _sigmoid(_mm1(y, gw_ref[...]) + gb_ref[...])
    y_ref[...] = y.astype(y_ref.dtype)


def _cmul(ar, ai, br, bi):
    return ar * br - ai * bi, ar * bi + ai * br


def _s5_params(lam_re, lam_im, b_re, b_im, c_re, c_im, log_dt, nblk):
    hp = lax.Precision.HIGHEST
    g, n = S5_GROUPS, S5_BLOCK
    dt = jnp.exp(log_dt)[:, None]
    mag = jnp.exp(lam_re * dt)
    lbr = mag * jnp.cos(lam_im * dt)
    lbi = mag * jnp.sin(lam_im * dt)
    den = lam_re * lam_re + lam_im * lam_im
    fr = ((lbr - 1.0) * lam_re + lbi * lam_im) / den
    fi = (lbi * lam_re - (lbr - 1.0) * lam_im) / den
    bbr = fr[..., None] * b_re - fi[..., None] * b_im
    bbi = fr[..., None] * b_im + fi[..., None] * b_re

    pws = [(jnp.ones_like(lbr), jnp.zeros_like(lbr))]
    for _ in range(n):
        pws.append(_cmul(pws[-1][0], pws[-1][1], lbr, lbi))
    pwr = jnp.stack([p[0] for p in pws])
    pwi = jnp.stack([p[1] for p in pws])
    clr, cli = _cmul(c_re[None], c_im[None], pwr[:, :, None, :], pwi[:, :, None, :])

    kmat = (jnp.einsum("kgcp,gpd->kgcd", clr[:n], bbr, precision=hp)
            - jnp.einsum("kgcp,gpd->kgcd", cli[:n], bbi, precision=hp))
    idx = jnp.arange(n)
    lag = idx[None, :] - idx[:, None]
    kt = jnp.transpose(kmat, (0, 1, 3, 2))[jnp.clip(lag, 0, n - 1)]
    mmat = jnp.where((lag >= 0)[:, :, None, None, None], kt, 0.0)
    mmat = jnp.transpose(mmat, (2, 0, 3, 1, 4)).reshape(g, MIX, MIX)

    rr, ri = pwr[n - 1::-1][:, :, None, :], pwi[n - 1::-1][:, :, None, :]
    btr, bti = jnp.transpose(bbr, (0, 2, 1))[None], jnp.transpose(bbi, (0, 2, 1))[None]
    er, ei = _cmul(rr, ri, btr, bti)
    er = jnp.transpose(er, (1, 0, 2, 3)).reshape(g, MIX, S5_STATE)
    ei = jnp.transpose(ei, (1, 0, 2, 3)).reshape(g, MIX, S5_STATE)
    z = jnp.zeros((g // 2, MIX, S5_STATE), F32)
    emat = jnp.concatenate([jnp.concatenate([er[0::2], z, ei[0::2], z], axis=-1),
                            jnp.concatenate([z, er[1::2], z, ei[1::2]], axis=-1)], axis=1)

    pcr = jnp.transpose(clr[1:n + 1], (1, 3, 0, 2)).reshape(g, S5_STATE, MIX)
    pci = -jnp.transpose(cli[1:n + 1], (1, 3, 0, 2)).reshape(g, S5_STATE, MIX)
    z = jnp.zeros((g // 2, S5_STATE, MIX), F32)
    pmat = jnp.stack([jnp.concatenate([pcr[0::2], z, pci[0::2], z], axis=1),
                      jnp.concatenate([z, pcr[1::2], z, pci[1::2]], axis=1)], axis=1).reshape(g, MIX, MIX)

    pr = pwr[n].reshape(1, S5_LANES)
    pi = pwi[n].reshape(1, S5_LANES)
    steps = []
    m = 1
    while m < nblk:
        fr_, fi_ = pr[m - 1:m], pi[m - 1:m]
        steps.append(jnp.concatenate([fr_, fi_], axis=1))
        pr, pi = (jnp.concatenate([pr, pr * fr_ - pi * fi_], axis=0),
                  jnp.concatenate([pi, pr * fi_ + pi * fr_], axis=0))
        m *= 2
    lstep = jnp.concatenate(steps, axis=0)
    lpow = jnp.concatenate([pr, pi], axis=1)
    return emat.astype(BF16), mmat.astype(BF16), pmat.astype(BF16), lstep, lpow


def _s5(u, lam_re, lam_im, b_re, b_im, c_re, c_im, d, log_dt, glu_w, glu_b):
    b, s, _ = u.shape
    g = S5_GROUPS
    nb_tot = s // S5_BLOCK
    nblk = min(S5_NBLK, nb_tot)
    emat, mmat, pmat, lstep, lpow = _s5_params(lam_re, lam_im, b_re, b_im, c_re, c_im, log_dt, nblk)
    nstep = lstep.shape[0]
    ug = u.reshape(b, nb_tot, S5_BLOCK, g, S5_GROUP).transpose(0, 3, 1, 2, 4).reshape(b, g, nb_tot, MIX)
    blk = pl.BlockSpec((None, g, nblk, MIX), lambda i, j: (i, 0, j, 0))
    ys = pl.pallas_call(
        functools.partial(_s5_ssm_kernel, nblk=nblk),
        grid=(b, nb_tot // nblk),
        in_specs=[blk,
                  _const_spec((g // 2, 2 * MIX, MIX)),
                  _const_spec((g, MIX, MIX)), _const_spec((g, MIX, MIX)),
                  _const_spec((nstep, 2 * S5_LANES)),
                  _const_spec((nblk, 2 * S5_LANES))],
        out_specs=blk,
        out_shape=jax.ShapeDtypeStruct((b, g, nb_tot, MIX), F32),
        scratch_shapes=[pltpu.VMEM((1, 2 * S5_LANES), F32)],
        compiler_params=_cparams(("parallel", "arbitrary")),
        name="s5_ssm",
    )(ug, emat, mmat, pmat, lstep, lpow)
    ys = ys.reshape(b, g, nb_tot, S5_BLOCK, S5_GROUP).transpose(0, 2, 3, 1, 4).reshape(b * s, MIX)
    t = b * s
    tm = min(ROW_TILE, t)
    out = pl.pallas_call(
        _s5_out_kernel,
        grid=(t // tm,),
        in_specs=[pl.BlockSpec((tm, MIX), lambda i: (i, 0)), pl.BlockSpec((tm, MIX), lambda i: (i, 0)),
                  _const_spec((1, MIX)), _const_spec((MIX, MIX)), _const_spec((1, MIX))],
        out_specs=pl.BlockSpec((tm, MIX), lambda i: (i, 0)),
        out_shape=jax.ShapeDtypeStruct((t, MIX), BF16),
        compiler_params=_cparams(("parallel",)),
        name="s5_out",
    )(ys, u.reshape(t, MIX), d.reshape(1, MIX), glu_w.astype(BF16), glu_b.reshape(1, MIX))
    return out.reshape(b, s, MIX)


def _rwkv_kernel(p_ref, mu_ref, w0_ref, wup_ref, a0_ref, aup_ref, gup_ref, kk_ref, ka_ref, rk_ref,
                 lng_ref, lnb_ref, y_ref, prev, s_scr, o_scr, *, tc, nb):
    @pl.when(pl.program_id(0) == 0)
    def _():
        prev[...] = jnp.zeros_like(prev)
        s_scr[...] = jnp.zeros_like(s_scr)

    hm = _HeadMasks()
    ones_bd = hm.ones_bd
    tril = _block_tril(tc, 6)
    nc = tc // CHUNK
    row = _iota2((tc, RWKV_COLS), 0)

    items = []
    post = []
    for bi in range(nb):
        p = p_ref[bi]
        shifted = jnp.where(row == 0, prev[bi], pltpu.roll(p, 1, 0))
        prev[bi] = p[tc - 1:tc, :]
        p = p + mu_ref[...] * (shifted - p)
        r = p[:, 0:MIX]
        k = p[:, MIX:2 * MIX]
        v = p[:, 2 * MIX:3 * MIX]
        o0 = 3 * MIX
        wd = p[:, o0:o0 + DECAY_LORA]
        ad = p[:, o0 + DECAY_LORA:o0 + DECAY_LORA + AAA_LORA]
        gd = p[:, o0 + DECAY_LORA + AAA_LORA:RWKV_COLS]

        logw = -_softplus(-(w0_ref[...] + _mm1(jnp.tanh(wd), wup_ref[...]))) - 0.5
        ld = -jnp.exp(logw)
        a = _sigmoid(a0_ref[...] + _mm1(ad, aup_ref[...]))
        g = _mm1(_sigmoid(gd), gup_ref[...])
        kk = k * kk_ref[...]
        kk = kk * lax.rsqrt(_seg_sum(kk * kk, ones_bd) + EPS)
        k = k * (1.0 + (a - 1.0) * ka_ref[...])
        akk = a * kk

        cum = _mm_exact_lhs(tril, ld)
        ncum = jnp.exp(-cum)
        r_t = r * jnp.exp(cum)
        b_t = kk * jnp.exp(cum - ld)
        k_t = k * ncum
        a_t = akk * ncum
        post.append((r, k, v, g))
        for c in range(nc):
            rows = slice(c * CHUNK, (c + 1) * CHUNK)
            clast = cum[(c + 1) * CHUNK - 1:(c + 1) * CHUNK, :]
            tail = jnp.exp(clast - cum[rows])
            items.append(dict(r=r_t[rows], b=b_t[rows], k=k_t[rows], a=a_t[rows], v=v[rows],
                              kend=k[rows] * tail, aend=akk[rows] * tail, plast=jnp.exp(clast)))

    for it in items:
        lhs = jnp.concatenate([it["b"], it["r"]], axis=0)
        x1 = _hmm(lhs, it["a"], hm, _NT, _P_A)
        x2 = _hmm(lhs, it["k"], hm, _NT, _P_A)
        it["a_ba"] = jnp.where(hm.strict, x1[:CHUNK], 0.0)
        it["a_ra"] = jnp.where(hm.causal, x1[CHUNK:], 0.0)
        a_bk = jnp.where(hm.strict, x2[:CHUNK], 0.0)
        a_rk = jnp.where(hm.causal, x2[CHUNK:], 0.0)
        av = _hmm(jnp.concatenate([a_bk, a_rk], axis=0), it["v"], hm, _NN, _P_A)
        it["abkv"] = av[:CHUNK]
        it["arkv"] = av[CHUNK:]
    tinvs = _inv_unit_lower([it["a_ba"] for it in items], hm)
    for it, tinv in zip(items, tinvs):
        it["tb"] = _hmm(tinv, it["b"], hm, _NN, _P_UW)
        it["tz0"] = _hmm(tinv, it["abkv"], hm, _NN, _P_UW)
        it["kaend"] = jnp.concatenate([it["kend"], it["aend"]], axis=0)

    states = [[s_scr[bi, pi] for pi in range(len(PAIRS))] for bi in range(nb)]
    for c in range(nc):
        for bi in range(nb):
            it = items[bi * nc + c]
            xs = _state_mm(jnp.concatenate([it["tb"], it["r"]], axis=0), states[bi], _NT)
            z = xs[:CHUNK] + it["tz0"]
            o = xs[CHUNK:] + it["arkv"] - _hmm(it["a_ra"], z, hm, _NN, _P_A)
            states[bi] = _state_update(states[bi], it["plast"], jnp.concatenate([it["v"], -z], axis=0),
                                       it["kaend"], hm)
            o_scr[bi, c * CHUNK:(c + 1) * CHUNK, :] = o
    for bi in range(nb):
        for pi in range(len(PAIRS)):
            s_scr[bi, pi] = states[bi][pi]

    inv_n = 1.0 / HEAD_DIM
    for bi in range(nb):
        r, k, v, g = post[bi]
        o = o_scr[bi]
        mean = _seg_sum(o, ones_bd) * inv_n
        cen = o - mean
        var = _seg_sum(cen * cen, ones_bd) * inv_n
        o = cen * lax.rsqrt(var + RWKV_LN_EPS) * lng_ref[...] + lnb_ref[...]
        bonus = _seg_sum(r * k * rk_ref[...], ones_bd) * v
        y_ref[bi] = ((o + bonus) * g).astype(y_ref.dtype)


def _rwkv(p, mu, w0, w_up, a0, a_up, g_up, k_k, k_a, r_k, ln_g, ln_b):
    b, s, _ = p.shape
    tc = min(TC_RWKV, s)
    row = lambda t: t.reshape(1, MIX)
    return pl.pallas_call(
        functools.partial(_rwkv_kernel, tc=tc, nb=b),
        grid=(s // tc,),
        in_specs=[pl.BlockSpec((b, tc, RWKV_COLS), lambda j: (0, j, 0)),
                  _const_spec((1, RWKV_COLS)),
                  _const_spec((1, MIX)), _const_spec((DECAY_LORA, MIX)),
                  _const_spec((1, MIX)), _const_spec((AAA_LORA, MIX)),
                  _const_spec((GATE_LORA, MIX)),
                  _const_spec((1, MIX)), _const_spec((1, MIX)), _const_spec((1, MIX)),
                  _const_spec((1, MIX)), _const_spec((1, MIX))],
        out_specs=pl.BlockSpec((b, tc, MIX), lambda j: (0, j, 0)),
        out_shape=jax.ShapeDtypeStruct((b, s, MIX), BF16),
        scratch_shapes=[pltpu.VMEM((b, 1, RWKV_COLS), F32),
                        pltpu.VMEM((b, len(PAIRS), PAIR, PAIR), F32),
                        pltpu.VMEM((b, tc, MIX), F32)],
        compiler_params=_cparams(("arbitrary",)),
        name="rwkv7_mixer",
    )(p, mu.reshape(1, RWKV_COLS), row(w0), w_up.astype(BF16), row(a0), a_up.astype(BF16),
      g_up.astype(BF16), row(k_k), row(k_a), row(r_k.reshape(MIX)), row(ln_g), row(ln_b))


def _merge_kernel(x_ref, g_ref, wg_ref, y0_ref, y1_ref, y2_ref, y3_ref, wb_ref, wo_ref, o_ref):
    x = x_ref[...]
    h = _rms(x, g_ref[...]).astype(BF16)
    merged = None
    for i, y_ref in enumerate((y0_ref, y1_ref, y2_ref, y3_ref)):
        gate = _sigmoid(_mm(h, wg_ref[:, i * D_MODEL:(i + 1) * D_MODEL]))
        term = gate * _mm(y_ref[...], wb_ref[i])
        merged = term if merged is None else merged + term
    o_ref[...] = x + _mm(merged.astype(BF16), wo_ref[...])


def _merge(x2, norm_g, w_gate, ys, w_branch, w_out):
    t = x2.shape[0]
    tm = min(ROW_TILE, t)
    return pl.pallas_call(
        _merge_kernel,
        grid=(t // tm,),
        in_specs=[pl.BlockSpec((tm, D_MODEL), lambda i: (i, 0)),
                  _const_spec((1, D_MODEL)),
                  _const_spec((D_MODEL, HEADS * D_MODEL))]
                 + [pl.BlockSpec((tm, MIX), lambda i: (i, 0)) for _ in range(4)]
                 + [_const_spec((4, MIX, D_MODEL)), _const_spec((D_MODEL, D_MODEL))],
        out_specs=pl.BlockSpec((tm, D_MODEL), lambda i: (i, 0)),
        out_shape=jax.ShapeDtypeStruct((t, D_MODEL), F32),
        compiler_params=_cparams(("parallel",)),
        name="gated_merge",
    )(x2, norm_g.reshape(1, D_MODEL), w_gate, *ys, w_branch, w_out)


def _mlp_kernel(x_ref, g_ref, w1_ref, w2_ref, gf_ref, o_ref, *, final_norm):
    x = x_ref[...]
    h = _rms(x, g_ref[...]).astype(BF16)
    a = jnp.maximum(_mm(h, w1_ref[...]), 0.0)
    x = x + _mm((a * a).astype(BF16), w2_ref[...])
    if final_norm:
        x = _rms(x, gf_ref[...])
    o_ref[...] = x


def _mlp(x2, norm_g, w1, w2, final_g, final_norm):
    t = x2.shape[0]
    tm = min(ROW_TILE, t)
    return pl.pallas_call(
        functools.partial(_mlp_kernel, final_norm=final_norm),
        grid=(t // tm,),
        in_specs=[pl.BlockSpec((tm, D_MODEL), lambda i: (i, 0)),
                  _const_spec((1, D_MODEL)),
                  _const_spec((D_MODEL, D_FF)), _const_spec((D_FF, D_MODEL)),
                  _const_spec((1, D_MODEL))],
        out_specs=pl.BlockSpec((tm, D_MODEL), lambda i: (i, 0)),
        out_shape=jax.ShapeDtypeStruct((t, D_MODEL), F32),
        compiler_params=_cparams(("parallel",)),
        name="mlp",
    )(x2, norm_g.reshape(1, D_MODEL), w1, w2, final_g.reshape(1, D_MODEL))


def _mix_weight(w_in):
    qkvz = w_in[:, 0:4 * MIX]
    gates = jnp.repeat(w_in[:, 4 * MIX:GDN_COLS], HEAD_DIM, axis=1)
    rest = w_in[:, GDN_COLS:GDN_COLS + LRU_COLS + S5_COLS + RWKV_COLS]
    return jnp.concatenate([qkvz, rest, gates], axis=1).astype(BF16)


def kernel(x, norm1_g, w_in, gdn_conv_w, gdn_a_log, gdn_dt_bias, gdn_norm_g, lru_conv_w, lru_conv_b, lru_w_a, lru_b_a, lru_w_x, lru_b_x, lru_lambda, s5_lambda_re, s5_lambda_im, s5_b_re, s5_b_im, s5_c_re, s5_c_im, s5_d, s5_log_dt, s5_glu_w, s5_glu_b, rwkv_mu, rwkv_w0, rwkv_w_up, rwkv_a0, rwkv_a_up, rwkv_g_up, rwkv_k_k, rwkv_k_a, rwkv_r_k, rwkv_ln_g, rwkv_ln_b, w_branch, w_out, norm2_g, mlp_w1, mlp_w2, final_norm_g):
    b, s, d = x.shape
    depth = w_in.shape[0]
    x2 = x.reshape(b * s, d)
    gate_off = GDN_COLS + LRU_COLS + S5_COLS + RWKV_COLS
    for l in range(depth):
        p_gdn, p_lru, p_s5, p_rwkv, p_ba = _in_proj(x2, norm1_g[l], _mix_weight(w_in[l]))
        shp = lambda t: t.reshape(b, s, t.shape[-1])
        ys = (
            _gdn(shp(p_gdn), shp(p_ba), gdn_conv_w[l], gdn_a_log[l], gdn_dt_bias[l], gdn_norm_g[l]),
            _lru(shp(p_lru), lru_conv_w[l], lru_conv_b[l], lru_w_a[l], lru_b_a[l], lru_w_x[l], lru_b_x[l],
                 lru_lambda[l]),
            _s5(shp(p_s5), s5_lambda_re[l], s5_lambda_im[l], s5_b_re[l], s5_b_im[l], s5_c_re[l], s5_c_im[l],
                s5_d[l], s5_log_dt[l], s5_glu_w[l], s5_glu_b[l]),
            _rwkv(shp(p_rwkv), rwkv_mu[l], rwkv_w0[l], rwkv_w_up[l], rwkv_a0[l], rwkv_a_up[l], rwkv_g_up[l],
                  rwkv_k_k[l], rwkv_k_a[l], rwkv_r_k[l], rwkv_ln_g[l], rwkv_ln_b[l]),
        )
        ys = tuple(y.reshape(b * s, MIX) for y in ys)
        x2 = _merge(x2, norm1_g[l], w_in[l][:, gate_off:].astype(BF16), ys,
                    w_branch[l].astype(BF16), w_out[l].astype(BF16))
        x2 = _mlp(x2, norm2_g[l], mlp_w1[l].astype(BF16), mlp_w2[l].astype(BF16), final_norm_g,
                  final_norm=(l == depth - 1))
    return x2.reshape(b, s, d)
```

```python
import functools
import math

import jax
import jax.numpy as jnp
from jax import lax
from jax.experimental import pallas as pl
from jax.experimental.pallas import tpu as pltpu

F32 = jnp.float32
BF16 = jnp.bfloat16

D_MODEL = 1024
MIX = 256
HEADS = 4
HEAD_DIM = 64
CHUNK = 64
CONV_W = 4
EPS = 1e-6
LRU_C = 8.0
S5_GROUPS = 16
S5_GROUP = 16
S5_STATE = 64
S5_LANES = S5_GROUPS * S5_STATE
DECAY_LORA = 64
AAA_LORA = 64
GATE_LORA = 128
RWKV_LN_EPS = 64e-5
D_FF = 4 * D_MODEL
LANE = 128
SUBLANE = 8
CONV_TAIL = SUBLANE

GDN_COLS = 4 * MIX + 2 * HEADS
LRU_COLS = 2 * MIX
S5_COLS = MIX
RWKV_COLS = 3 * MIX + DECAY_LORA + AAA_LORA + GATE_LORA
GDN_GATE_COLS = 2 * MIX
MIX_COLS = 4 * MIX + LRU_COLS + S5_COLS + RWKV_COLS + GDN_GATE_COLS

ROW_TILE = 512
TC_GDN = 256
TC_RWKV = 256
TC_LRU = 256
S5_BLOCK = 8
S5_NBLK = 256
VMEM_LIMIT = 56 * 1024 * 1024

_P_QK = 1
_P_A = 1
_P_INV = 1
_P_UW = 1
_P_STATE = 1

_NN = (((1,), (0,)), ((), ()))
_NT = (((1,), (1,)), ((), ()))
_TN = (((0,), (0,)), ((), ()))


def _mm(a, b, dims=_NN):
    return lax.dot_general(a, b, dims, preferred_element_type=F32)


def _mm1(a, b, dims=_NN):
    return _mm(a.astype(BF16), b.astype(BF16), dims)


def _split2(x):
    hi = x.astype(BF16)
    lo = (x - hi.astype(F32)).astype(BF16)
    return hi, lo


def _mm_exact_lhs(a_bf, x):
    x1 = x.astype(BF16)
    r1 = x - x1.astype(F32)
    x2 = r1.astype(BF16)
    x3 = (r1 - x2.astype(F32)).astype(BF16)
    return _mm(a_bf, x1) + (_mm(a_bf, x2) + _mm(a_bf, x3))


def _seg_sum(x, ones_bd):
    hi, lo = _split2(x)
    return _mm(hi, ones_bd) + _mm(lo, ones_bd)


def _iota2(shape, dim):
    return lax.broadcasted_iota(jnp.int32, shape, dim)


def _block_ones(n, shift):
    r = jnp.right_shift(_iota2((n, n), 0), shift)
    c = jnp.right_shift(_iota2((n, n), 1), shift)
    return jnp.where(r == c, 1.0, 0.0).astype(BF16)


def _block_tril(n, shift):
    ri = _iota2((n, n), 0)
    ci = _iota2((n, n), 1)
    same = jnp.right_shift(ri, shift) == jnp.right_shift(ci, shift)
    return jnp.where(same & (ci <= ri), 1.0, 0.0).astype(BF16)


def _softplus(x):
    return jnp.maximum(x, 0.0) + jnp.log1p(jnp.exp(-jnp.abs(x)))


def _sigmoid(x):
    return 1.0 / (1.0 + jnp.exp(-x))


def _silu(x):
    return x * _sigmoid(x)


def _gelu_tanh(x):
    c = math.sqrt(2.0 / math.pi)
    return 0.5 * x * (1.0 + jnp.tanh(c * (x + 0.044715 * (x * x * x))))


def _rms(x, g):
    return x * lax.rsqrt(jnp.mean(x * x, axis=-1, keepdims=True) + EPS) * g


PAIR = 2 * HEAD_DIM
PAIRS = (slice(0, PAIR), slice(PAIR, 2 * PAIR))


class _HeadMasks:
    def __init__(self):
        ri = _iota2((CHUNK, MIX), 0)
        ci = _iota2((CHUNK, MIX), 1) & (HEAD_DIM - 1)
        self.strict = ci < ri
        self.causal = ci <= ri
        self.upper = ri <= ci
        self.base = self.strict & (jnp.right_shift(ri, 3) == jnp.right_shift(ci, 3))
        self.levels = []
        for sh in (3, 4, 5):
            rb = jnp.right_shift(ri, sh)
            cb = jnp.right_shift(ci, sh)
            self.levels.append((rb == cb + 1) & ((rb & 1) == 1))
        self.eye = jnp.where(ri == ci, 1.0, 0.0).astype(F32)
        first = _iota2((CHUNK, PAIR), 1) < HEAD_DIM
        self.lane_lo = jnp.where(first, 1.0, 0.0).astype(BF16)
        self.lane_hi = jnp.where(first, 0.0, 1.0).astype(BF16)
        r2 = jnp.right_shift(_iota2((PAIR, PAIR), 0), 6)
        c2 = jnp.right_shift(_iota2((PAIR, PAIR), 1), 6)
        self.bd = r2 == c2
        self.ones_bd = _block_ones(MIX, 6)


def _embed(xb, hm):
    return jnp.concatenate([xb * hm.lane_lo, xb * hm.lane_hi], axis=0)


def _hmm(a, b, hm, dims=_NN, passes=3):
    m = a.shape[0]
    outs = []
    for s in PAIRS:
        if passes == 1:
            outs.append(_mm(a[:, s].astype(BF16), _embed(b[:, s].astype(BF16), hm), dims))
        else:
            ah, al = _split2(a[:, s])
            bh, bl = _split2(b[:, s])
            top = _mm(jnp.concatenate([ah, al], axis=0), _embed(bh, hm), dims)
            outs.append(top[:m] + (top[m:] + _mm(ah, _embed(bl, hm), dims)))
    return jnp.concatenate(outs, axis=1)


def _inv_unit_lower(lmats, hm):
    p = _P_INV
    ld = [jnp.where(hm.base, l, 0.0) for l in lmats]
    l2 = [_hmm(x, x, hm, _NN, p) for x in ld]
    l4 = [_hmm(x, x, hm, _NN, p) for x in l2]
    t = [hm.eye - x for x in ld]
    t = [ti + _hmm(ti, x, hm, _NN, p) for ti, x in zip(t, l2)]
    t = [ti + _hmm(ti, x, hm, _NN, p) for ti, x in zip(t, l4)]
    for m in hm.levels:
        off = [jnp.where(m, l, 0.0) for l in lmats]
        ta = [_hmm(ti, x, hm, _NN, p) for ti, x in zip(t, off)]
        t = [ti - _hmm(x, ti, hm, _NN, p) for ti, x in zip(t, ta)]
    return t


def _state_mm(lhs, state, dims=_NN):
    outs = []
    for s, st in zip(PAIRS, state):
        lh = lhs[:, s].astype(BF16)
        sh = st.astype(BF16)
        out = _mm(lh, sh, dims)
        if _P_STATE >= 2:
            out = out + _mm(lh, (st - sh.astype(F32)).astype(BF16), dims)
        if _P_STATE >= 3:
            out = out + _mm((lhs[:, s] - lh.astype(F32)).astype(BF16), sh, dims)
        outs.append(out)
    return jnp.concatenate(outs, axis=1)


def _state_update(state, decay_row, lhs, rhs, hm):
    new = []
    for s, st in zip(PAIRS, state):
        upd = _mm(lhs[:, s].astype(BF16), rhs[:, s].astype(BF16), _TN)
        new.append(st * decay_row[:, s] + jnp.where(hm.bd, upd, 0.0))
    return new


def _shift_rows(x, sh, fill, row):
    if sh % SUBLANE == 0:
        return jnp.concatenate([jnp.full((sh, x.shape[1]), fill, x.dtype), x[:x.shape[0] - sh]], axis=0)
    return jnp.where(row >= sh, pltpu.roll(x, sh, 0), fill)


def _causal_conv(xbuf, cur, w_ref, tc):
    xbuf[CONV_TAIL:CONV_TAIL + tc, :] = cur
    acc = None
    for i in range(CONV_W):
        off = CONV_TAIL - (CONV_W - 1) + i
        term = w_ref[i:i + 1, :] * xbuf[off:off + tc, :]
        acc = term if acc is None else acc + term
    xbuf[0:CONV_TAIL, :] = xbuf[tc:tc + CONV_TAIL, :]
    return acc


def _cparams(sem):
    return pltpu.CompilerParams(dimension_semantics=sem, vmem_limit_bytes=VMEM_LIMIT)


def _const_spec(shape):
    nd = len(shape)
    return pl.BlockSpec(shape, lambda *_: (0,) * nd)


def _inproj_kernel(x_ref, g_ref, w_ref, gdn_ref, lru_ref, s5_ref, rwkv_ref, ba_ref):
    h = _rms(x_ref[...], g_ref[...]).astype(BF16)
    p = _mm(h, w_ref[...])
    o = 0
    for ref in (gdn_ref, lru_ref, s5_ref, rwkv_ref, ba_ref):
        wdt = ref.shape[-1]
        ref[...] = p[:, o:o + wdt]
        o += wdt


def _in_proj(x2, norm_g, w_mix):
    t = x2.shape[0]
    tm = min(ROW_TILE, t)
    widths = (4 * MIX, LRU_COLS, S5_COLS, RWKV_COLS, GDN_GATE_COLS)
    return pl.pallas_call(
        _inproj_kernel,
        grid=(t // tm,),
        in_specs=[pl.BlockSpec((tm, D_MODEL), lambda i: (i, 0)),
                  _const_spec((1, D_MODEL)),
                  _const_spec((D_MODEL, MIX_COLS))],
        out_specs=[pl.BlockSpec((tm, w), lambda i: (i, 0)) for w in widths],
        out_shape=[jax.ShapeDtypeStruct((t, w), F32) for w in widths],
        compiler_params=_cparams(("parallel",)),
        name="in_proj",
    )(x2, norm_g.reshape(1, D_MODEL), w_mix)


def _gdn_kernel(p_ref, gl_ref, cw_ref, alog_ref, dtb_ref, ng_ref, y_ref, xbuf, s_scr, o_scr, *, tc, nb):
    @pl.when(pl.program_id(0) == 0)
    def _():
        xbuf[:, 0:CONV_TAIL, :] = jnp.zeros((nb, CONV_TAIL, 3 * MIX), F32)
        s_scr[...] = jnp.zeros_like(s_scr)

    hm = _HeadMasks()
    ones_bd = hm.ones_bd
    ones_cc = jnp.ones((CHUNK, CHUNK), BF16)
    tril = _block_tril(tc, 6)
    nc = tc // CHUNK

    items = []
    for bi in range(nb):
        qkv = _silu(_causal_conv(xbuf.at[bi], p_ref[bi, :, 0:3 * MIX], cw_ref, tc))
        q = qkv[:, 0:MIX]
        k = qkv[:, MIX:2 * MIX]
        v = qkv[:, 2 * MIX:3 * MIX]
        q = q * lax.rsqrt(_seg_sum(q * q, ones_bd) + EPS) * (HEAD_DIM ** -0.5)
        k = k * lax.rsqrt(_seg_sum(k * k, ones_bd) + EPS)
        beta = _sigmoid(gl_ref[bi, :, 0:MIX])
        g = -jnp.exp(alog_ref[...]) * _softplus(gl_ref[bi, :, MIX:2 * MIX] + dtb_ref[...])
        gc = _mm_exact_lhs(tril, g)
        for c in range(nc):
            rows = slice(c * CHUNK, (c + 1) * CHUNK)
            items.append(dict(bi=bi, c=c, q=q[rows], k=k[rows], v=v[rows], beta=beta[rows],
                              g=g[rows], gc=gc[rows]))

    for it in items:
        g2 = _mm_exact_lhs(ones_cc, jnp.where(hm.upper, it["g"], 0.0))
        diff = jnp.where(hm.causal, it["gc"] - g2, 0.0)
        it["dmat"] = jnp.where(hm.causal, jnp.exp(diff), 0.0)
        it["kb"] = it["k"] * it["beta"]
    for it in items:
        kk = _hmm(it["kb"], it["k"], hm, _NT, _P_QK)
        it["lmat"] = jnp.where(hm.strict, kk * it["dmat"], 0.0)
    tinvs = _inv_unit_lower([it["lmat"] for it in items], hm)
    for it, tinv in zip(items, tinvs):
        eg = jnp.exp(it["gc"])
        it["u"] = _hmm(tinv, it["v"] * it["beta"], hm, _NN, _P_UW)
        it["w"] = _hmm(tinv, it["kb"] * eg, hm, _NN, _P_UW)
        it["attn"] = _hmm(it["q"], it["k"], hm, _NT, _P_QK) * it["dmat"]
        it["qd"] = it["q"] * eg
        glast = it["gc"][CHUNK - 1:CHUNK, :]
        it["kdec"] = it["k"] * jnp.exp(glast - it["gc"])
        it["elast"] = jnp.exp(glast)

    states = [[s_scr[bi, pi] for pi in range(len(PAIRS))] for bi in range(nb)]
    for c in range(nc):
        for bi in range(nb):
            it = items[bi * nc + c]
            ws = _state_mm(jnp.concatenate([it["w"], it["qd"]], axis=0), states[bi])
            vnew = it["u"] - ws[:CHUNK]
            o = ws[CHUNK:] + _hmm(it["attn"], vnew, hm, _NN, 1)
            states[bi] = _state_update(states[bi], it["elast"], it["kdec"], vnew, hm)
            o_scr[bi, c * CHUNK:(c + 1) * CHUNK, :] = o
    for bi in range(nb):
        for pi in range(len(PAIRS)):
            s_scr[bi, pi] = states[bi][pi]

    for bi in range(nb):
        o = o_scr[bi]
        o = o * lax.rsqrt(_seg_sum(o * o, ones_bd) * (1.0 / HEAD_DIM) + EPS) * ng_ref[...]
        y_ref[bi] = (o * _silu(p_ref[bi, :, 3 * MIX:4 * MIX])).astype(y_ref.dtype)


def _head_row(t):
    return jnp.repeat(t, HEAD_DIM).reshape(1, MIX)


def _gdn(p, gl, conv_w, a_log, dt_bias, norm_g):
    b, s, _ = p.shape
    tc = min(TC_GDN, s)
    ng_row = jnp.tile(norm_g, HEADS).reshape(1, MIX)
    return pl.pallas_call(
        functools.partial(_gdn_kernel, tc=tc, nb=b),
        grid=(s // tc,),
        in_specs=[pl.BlockSpec((b, tc, 4 * MIX), lambda j: (0, j, 0)),
                  pl.BlockSpec((b, tc, GDN_GATE_COLS), lambda j: (0, j, 0)),
                  _const_spec((CONV_W, 3 * MIX)),
                  _const_spec((1, MIX)), _const_spec((1, MIX)), _const_spec((1, MIX))],
        out_specs=pl.BlockSpec((b, tc, MIX), lambda j: (0, j, 0)),
        out_shape=jax.ShapeDtypeStruct((b, s, MIX), BF16),
        scratch_shapes=[pltpu.VMEM((b, tc + CONV_TAIL, 3 * MIX), F32),
                        pltpu.VMEM((b, len(PAIRS), PAIR, PAIR), F32),
                        pltpu.VMEM((b, tc, MIX), F32)],
        compiler_params=_cparams(("arbitrary",)),
        name="gdn_mixer",
    )(p, gl, conv_w, _head_row(a_log), _head_row(dt_bias), ng_row)


def _lru_kernel(p_ref, cw_ref, cb_ref, wa_ref, ba_ref, wx_ref, bx_ref, lam_ref, y_ref, xbuf, hcar, *, tc):
    @pl.when(pl.program_id(1) == 0)
    def _():
        xbuf[0:CONV_TAIL, :] = jnp.zeros((CONV_TAIL, MIX), F32)
        hcar[...] = jnp.zeros_like(hcar)

    u = _causal_conv(xbuf, p_ref[:, 0:MIX], cw_ref, tc) + cb_ref[...]
    ub = u.astype(BF16)
    r = _sigmoid(_mm(ub, wa_ref[...]) + ba_ref[...])
    i = _sigmoid(_mm(ub, wx_ref[...]) + bx_ref[...])
    log_a = (-LRU_C) * r * _softplus(-lam_ref[...])
    a = jnp.exp(log_a)
    inp = jnp.sqrt(-jnp.tanh(log_a) * (a * a + 1.0)) * (i * u)

    row = _iota2((tc, MIX), 0)
    acc_a, acc_b = a, inp
    sh = 1
    while sh < tc:
        a_sh = _shift_rows(acc_a, sh, 1.0, row)
        b_sh = _shift_rows(acc_b, sh, 0.0, row)
        acc_b = acc_b + acc_a * b_sh
        acc_a = acc_a * a_sh
        sh *= 2
    h = acc_b + acc_a * hcar[...]
    hcar[...] = h[tc - 1:tc, :]
    y_ref[...] = (h * _gelu_tanh(p_ref[:, MIX:2 * MIX])).astype(y_ref.dtype)


def _block_diag(w):
    n, d, e = w.shape
    eye = jnp.eye(n, dtype=w.dtype)
    return (eye[:, None, :, None] * w[:, :, None, :]).reshape(n * d, n * e)


def _lru(p, conv_w, conv_b, w_a, b_a, w_x, b_x, lam):
    b, s, _ = p.shape
    tc = min(TC_LRU, s)
    row = lambda t: t.reshape(1, MIX)
    return pl.pallas_call(
        functools.partial(_lru_kernel, tc=tc),
        grid=(b, s // tc),
        in_specs=[pl.BlockSpec((None, tc, LRU_COLS), lambda i, j: (i, j, 0)),
                  _const_spec((CONV_W, MIX)), _const_spec((1, MIX)),
                  _const_spec((MIX, MIX)), _const_spec((1, MIX)),
                  _const_spec((MIX, MIX)), _const_spec((1, MIX)), _const_spec((1, MIX))],
        out_specs=pl.BlockSpec((None, tc, MIX), lambda i, j: (i, j, 0)),
        out_shape=jax.ShapeDtypeStruct((b, s, MIX), BF16),
        scratch_shapes=[pltpu.VMEM((tc + CONV_TAIL, MIX), F32), pltpu.VMEM((1, MIX), F32)],
        compiler_params=_cparams(("parallel", "arbitrary")),
        name="rglru_mixer",
    )(p, conv_w, row(conv_b), _block_diag(w_a).astype(BF16), row(b_a),
      _block_diag(w_x).astype(BF16), row(b_x), row(lam))


def _s5_kernel(u_ref, e_ref, k_ref, p_ref, lstep_ref, lpow_ref, d_ref, gw_ref, gb_ref, y_ref, hcar, *, nblk):
    @pl.when(pl.program_id(1) == 0)
    def _():
        hcar[...] = jnp.zeros_like(hcar)

    ub = [u_ref[:, j * MIX:(j + 1) * MIX].astype(BF16) for j in range(S5_BLOCK)]
    e = None
    for j in range(S5_BLOCK):
        t = _mm(ub[j], e_ref[j])
        e = t if e is None else e + t
    hr = e[:, 0:S5_LANES]
    hi = e[:, S5_LANES:2 * S5_LANES]

    row = _iota2((nblk, S5_LANES), 0)
    sh, kk = 1, 0
    while sh < nblk:
        sr = _shift_rows(hr, sh, 0.0, row)
        si = _shift_rows(hi, sh, 0.0, row)
        lr = lstep_ref[kk:kk + 1, 0:S5_LANES]
        li = lstep_ref[kk:kk + 1, S5_LANES:2 * S5_LANES]
        hr, hi = hr + (lr * sr - li * si), hi + (lr * si + li * sr)
        sh *= 2
        kk += 1
    cr = hcar[:, 0:S5_LANES]
    ci = hcar[:, S5_LANES:2 * S5_LANES]
    pr = lpow_ref[:, 0:S5_LANES]
    pi = lpow_ref[:, S5_LANES:2 * S5_LANES]
    hr, hi = hr + (pr * cr - pi * ci), hi + (pr * ci + pi * cr)
    hcar[:, 0:S5_LANES] = hr[nblk - 1:nblk, :]
    hcar[:, S5_LANES:2 * S5_LANES] = hi[nblk - 1:nblk, :]
    first = row == 0
    hr = jnp.where(first, cr, pltpu.roll(hr, 1, 0))
    hi = jnp.where(first, ci, pltpu.roll(hi, 1, 0))

    h_hi, h_lo = _split2(jnp.concatenate([hr, hi], axis=1))
    for i in range(S5_BLOCK):
        pm = p_ref[i]
        y = _mm(h_hi, pm) + _mm(h_lo, pm)
        for j in range(i + 1):
            y = y + _mm(ub[j], k_ref[i - j])
        y = _gelu_tanh(y + d_ref[...] * u_ref[:, i * MIX:(i + 1) * MIX])
        y = y * _sigmoid(_mm1(y, gw_ref[...]) + gb_ref[...])
        y_ref[:, i * MIX:(i + 1) * MIX] = y.astype(y_ref.dtype)


def _cmul(ar, ai, br, bi):
    return ar * br - ai * bi, ar * bi + ai * br


def _group_embed(x):
    n, g, a, b = x.shape
    eye = jnp.eye(g, dtype=x.dtype)
    return (eye[None, :, None, :, None] * x[:, :, :, None, :]).reshape(n, g * a, g * b)


def _s5_params(lam_re, lam_im, b_re, b_im, c_re, c_im, log_dt, nblk):
    hp = lax.Precision.HIGHEST
    n = S5_BLOCK
    dt = jnp.exp(log_dt)[:, None]
    mag = jnp.exp(lam_re * dt)
    lbr = mag * jnp.cos(lam_im * dt)
    lbi = mag * jnp.sin(lam_im * dt)
    den = lam_re * lam_re + lam_im * lam_im
    fr = ((lbr - 1.0) * lam_re + lbi * lam_im) / den
    fi = (lbi * lam_re - (lbr - 1.0) * lam_im) / den
    bbr = fr[..., None] * b_re - fi[..., None] * b_im
    bbi = fr[..., None] * b_im + fi[..., None] * b_re

    pws = [(jnp.ones_like(lbr), jnp.zeros_like(lbr))]
    for _ in range(n):
        pws.append(_cmul(pws[-1][0], pws[-1][1], lbr, lbi))
    pwr = jnp.stack([p[0] for p in pws])
    pwi = jnp.stack([p[1] for p in pws])
    clr, cli = _cmul(c_re[None], c_im[None], pwr[:, :, None, :], pwi[:, :, None, :])

    kmat = (jnp.einsum("kgcp,gpd->kgdc", clr[:n], bbr, precision=hp)
            - jnp.einsum("kgcp,gpd->kgdc", cli[:n], bbi, precision=hp))
    kbd = _group_embed(kmat)

    rr, ri = pwr[n - 1::-1][:, :, None, :], pwi[n - 1::-1][:, :, None, :]
    btr, bti = jnp.transpose(bbr, (0, 2, 1))[None], jnp.transpose(bbi, (0, 2, 1))[None]
    er, ei = _cmul(rr, ri, btr, bti)
    emat = jnp.concatenate([_group_embed(er), _group_embed(ei)], axis=2)

    pcr = jnp.transpose(clr[1:n + 1], (0, 1, 3, 2))
    pci = -jnp.transpose(cli[1:n + 1], (0, 1, 3, 2))
    pmat = jnp.concatenate([_group_embed(pcr), _group_embed(pci)], axis=1)

    pr = pwr[n].reshape(1, S5_LANES)
    pi = pwi[n].reshape(1, S5_LANES)
    steps = []
    m = 1
    while m < nblk:
        fr_, fi_ = pr[m - 1:m], pi[m - 1:m]
        steps.append(jnp.concatenate([fr_, fi_], axis=1))
        pr, pi = (jnp.concatenate([pr, pr * fr_ - pi * fi_], axis=0),
                  jnp.concatenate([pi, pr * fi_ + pi * fr_], axis=0))
        m *= 2
    lstep = jnp.concatenate(steps, axis=0)
    lpow = jnp.concatenate([pr, pi], axis=1)
    return emat.astype(BF16), kbd.astype(BF16), pmat.astype(BF16), lstep, lpow


def _s5(u, lam_re, lam_im, b_re, b_im, c_re, c_im, d, log_dt, glu_w, glu_b):
    b, s, _ = u.shape
    n = S5_BLOCK
    nb_tot = s // n
    nblk = min(S5_NBLK, nb_tot)
    emat, kbd, pmat, lstep, lpow = _s5_params(lam_re, lam_im, b_re, b_im, c_re, c_im, log_dt, nblk)
    nstep = lstep.shape[0]
    blk = pl.BlockSpec((None, nblk, n * MIX), lambda i, j: (i, j, 0))
    out = pl.pallas_call(
        functools.partial(_s5_kernel, nblk=nblk),
        grid=(b, nb_tot // nblk),
        in_specs=[blk,
                  _const_spec((n, MIX, 2 * S5_LANES)),
                  _const_spec((n, MIX, MIX)),
                  _const_spec((n, 2 * S5_LANES, MIX)),
                  _const_spec((nstep, 2 * S5_LANES)),
                  _const_spec((nblk, 2 * S5_LANES)),
                  _const_spec((1, MIX)), _const_spec((MIX, MIX)), _const_spec((1, MIX))],
        out_specs=blk,
        out_shape=jax.ShapeDtypeStruct((b, nb_tot, n * MIX), BF16),
        scratch_shapes=[pltpu.VMEM((1, 2 * S5_LANES), F32)],
        compiler_params=_cparams(("parallel", "arbitrary")),
        name="s5_mixer",
    )(u.reshape(b, nb_tot, n * MIX), emat, kbd, pmat, lstep, lpow,
      d.reshape(1, MIX), glu_w.astype(BF16), glu_b.reshape(1, MIX))
    return out.reshape(b, s, MIX)


def _rwkv_kernel(p_ref, mu_ref, w0_ref, wup_ref, a0_ref, aup_ref, gup_ref, kk_ref, ka_ref, rk_ref,
                 lng_ref, lnb_ref, y_ref, prev, s_scr, o_scr, *, tc, nb):
    @pl.when(pl.program_id(0) == 0)
    def _():
        prev[...] = jnp.zeros_like(prev)
        s_scr[...] = jnp.zeros_like(s_scr)

    hm = _HeadMasks()
    ones_bd = hm.ones_bd
    tril = _block_tril(tc, 6)
    nc = tc // CHUNK
    row = _iota2((tc, RWKV_COLS), 0)

    items = []
    post = []
    for bi in range(nb):
        p = p_ref[bi]
        shifted = jnp.where(row == 0, prev[bi], pltpu.roll(p, 1, 0))
        prev[bi] = p[tc - 1:tc, :]
        p = p + mu_ref[...] * (shifted - p)
        r = p[:, 0:MIX]
        k = p[:, MIX:2 * MIX]
        v = p[:, 2 * MIX:3 * MIX]
        o0 = 3 * MIX
        wd = p[:, o0:o0 + DECAY_LORA]
        ad = p[:, o0 + DECAY_LORA:o0 + DECAY_LORA + AAA_LORA]
        gd = p[:, o0 + DECAY_LORA + AAA_LORA:RWKV_COLS]

        logw = -_softplus(-(w0_ref[...] + _mm1(jnp.tanh(wd), wup_ref[...]))) - 0.5
        ld = -jnp.exp(logw)
        a = _sigmoid(a0_ref[...] + _mm1(ad, aup_ref[...]))
        g = _mm1(_sigmoid(gd), gup_ref[...])
        kk = k * kk_ref[...]
        kk = kk * lax.rsqrt(_seg_sum(kk * kk, ones_bd) + EPS)
        k = k * (1.0 + (a - 1.0) * ka_ref[...])
        akk = a * kk

        cum = _mm_exact_lhs(tril, ld)
        ncum = jnp.exp(-cum)
        r_t = r * jnp.exp(cum)
        b_t = kk * jnp.exp(cum - ld)
        k_t = k * ncum
        a_t = akk * ncum
        post.append((r, k, v, g))
        for c in range(nc):
            rows = slice(c * CHUNK, (c + 1) * CHUNK)
            clast = cum[(c + 1) * CHUNK - 1:(c + 1) * CHUNK, :]
            tail = jnp.exp(clast - cum[rows])
            items.append(dict(r=r_t[rows], b=b_t[rows], k=k_t[rows], a=a_t[rows], v=v[rows],
                              kend=k[rows] * tail, aend=akk[rows] * tail, plast=jnp.exp(clast)))

    for it in items:
        lhs = jnp.concatenate([it["b"], it["r"]], axis=0)
        x1 = _hmm(lhs, it["a"], hm, _NT, _P_A)
        x2 = _hmm(lhs, it["k"], hm, _NT, _P_A)
        it["a_ba"] = jnp.where(hm.strict, x1[:CHUNK], 0.0)
        it["a_ra"] = jnp.where(hm.causal, x1[CHUNK:], 0.0)
        a_bk = jnp.where(hm.strict, x2[:CHUNK], 0.0)
        a_rk = jnp.where(hm.causal, x2[CHUNK:], 0.0)
        av = _hmm(jnp.concatenate([a_bk, a_rk], axis=0), it["v"], hm, _NN, _P_A)
        it["abkv"] = av[:CHUNK]
        it["arkv"] = av[CHUNK:]
    tinvs = _inv_unit_lower([it["a_ba"] for it in items], hm)
    for it, tinv in zip(items, tinvs):
        it["tb"] = _hmm(tinv, it["b"], hm, _NN, _P_UW)
        it["tz0"] = _hmm(tinv, it["abkv"], hm, _NN, _P_UW)
        it["kaend"] = jnp.concatenate([it["kend"], it["aend"]], axis=0)

    states = [[s_scr[bi, pi] for pi in range(len(PAIRS))] for bi in range(nb)]
    for c in range(nc):
        for bi in range(nb):
            it = items[bi * nc + c]
            xs = _state_mm(jnp.concatenate([it["tb"], it["r"]], axis=0), states[bi], _NT)
            z = xs[:CHUNK] + it["tz0"]
            o = xs[CHUNK:] + it["arkv"] - _hmm(it["a_ra"], z, hm, _NN, _P_A)
            states[bi] = _state_update(states[bi], it["plast"], jnp.concatenate([it["v"], -z], axis=0),
                                       it["kaend"], hm)
            o_scr[bi, c * CHUNK:(c + 1) * CHUNK, :] = o
    for bi in range(nb):
        for pi in range(len(PAIRS)):
            s_scr[bi, pi] = states[bi][pi]

    inv_n = 1.0 / HEAD_DIM
    for bi in range(nb):
        r, k, v, g = post[bi]
        o = o_scr[bi]
        mean = _seg_sum(o, ones_bd) * inv_n
        cen = o - mean
        var = _seg_sum(cen * cen, ones_bd) * inv_n
        o = cen * lax.rsqrt(var + RWKV_LN_EPS) * lng_ref[...] + lnb_ref[...]
        bonus = _seg_sum(r * k * rk_ref[...], ones_bd) * v
        y_ref[bi] = ((o + bonus) * g).astype(y_ref.dtype)


def _rwkv(p, mu, w0, w_up, a0, a_up, g_up, k_k, k_a, r_k, ln_g, ln_b):
    b, s, _ = p.shape
    tc = min(TC_RWKV, s)
    row = lambda t: t.reshape(1, MIX)
    return pl.pallas_call(
        functools.partial(_rwkv_kernel, tc=tc, nb=b),
        grid=(s // tc,),
        in_specs=[pl.BlockSpec((b, tc, RWKV_COLS), lambda j: (0, j, 0)),
                  _const_spec((1, RWKV_COLS)),
                  _const_spec((1, MIX)), _const_spec((DECAY_LORA, MIX)),
                  _const_spec((1, MIX)), _const_spec((AAA_LORA, MIX)),
                  _const_spec((GATE_LORA, MIX)),
                  _const_spec((1, MIX)), _const_spec((1, MIX)), _const_spec((1, MIX)),
                  _const_spec((1, MIX)), _const_spec((1, MIX))],
        out_specs=pl.BlockSpec((b, tc, MIX), lambda j: (0, j, 0)),
        out_shape=jax.ShapeDtypeStruct((b, s, MIX), BF16),
        scratch_shapes=[pltpu.VMEM((b, 1, RWKV_COLS), F32),
                        pltpu.VMEM((b, len(PAIRS), PAIR, PAIR), F32),
                        pltpu.VMEM((b, tc, MIX), F32)],
        compiler_params=_cparams(("arbitrary",)),
        name="rwkv7_mixer",
    )(p, mu.reshape(1, RWKV_COLS), row(w0), w_up.astype(BF16), row(a0), a_up.astype(BF16),
      g_up.astype(BF16), row(k_k), row(k_a), row(r_k.reshape(MIX)), row(ln_g), row(ln_b))


def _merge_kernel(x_ref, g_ref, wg_ref, y0_ref, y1_ref, y2_ref, y3_ref, wb_ref, wo_ref, o_ref):
    x = x_ref[...]
    h = _rms(x, g_ref[...]).astype(BF16)
    merged = None
    for i, y_ref in enumerate((y0_ref, y1_ref, y2_ref, y3_ref)):
        gate = _sigmoid(_mm(h, wg_ref[:, i * D_MODEL:(i + 1) * D_MODEL]))
        term = gate * _mm(y_ref[...], wb_ref[i])
        merged = term if merged is None else merged + term
    o_ref[...] = x + _mm(merged.astype(BF16), wo_ref[...])


def _merge(x2, norm_g, w_gate, ys, w_branch, w_out):
    t = x2.shape[0]
    tm = min(ROW_TILE, t)
    return pl.pallas_call(
        _merge_kernel,
        grid=(t // tm,),
        in_specs=[pl.BlockSpec((tm, D_MODEL), lambda i: (i, 0)),
                  _const_spec((1, D_MODEL)),
                  _const_spec((D_MODEL, HEADS * D_MODEL))]
                 + [pl.BlockSpec((tm, MIX), lambda i: (i, 0)) for _ in range(4)]
                 + [_const_spec((4, MIX, D_MODEL)), _const_spec((D_MODEL, D_MODEL))],
        out_specs=pl.BlockSpec((tm, D_MODEL), lambda i: (i, 0)),
        out_shape=jax.ShapeDtypeStruct((t, D_MODEL), F32),
        compiler_params=_cparams(("parallel",)),
        name="gated_merge",
    )(x2, norm_g.reshape(1, D_MODEL), w_gate, *ys, w_branch, w_out)


def _mlp_kernel(x_ref, g_ref, w1_ref, w2_ref, gf_ref, o_ref, *, final_norm):
    x = x_ref[...]
    h = _rms(x, g_ref[...]).astype(BF16)
    a = jnp.maximum(_mm(h, w1_ref[...]), 0.0)
    x = x + _mm((a * a).astype(BF16), w2_ref[...])
    if final_norm:
        x = _rms(x, gf_ref[...])
    o_ref[...] = x


def _mlp(x2, norm_g, w1, w2, final_g, final_norm):
    t = x2.shape[0]
    tm = min(ROW_TILE, t)
    return pl.pallas_call(
        functools.partial(_mlp_kernel, final_norm=final_norm),
        grid=(t // tm,),
        in_specs=[pl.BlockSpec((tm, D_MODEL), lambda i: (i, 0)),
                  _const_spec((1, D_MODEL)),
                  _const_spec((D_MODEL, D_FF)), _const_spec((D_FF, D_MODEL)),
                  _const_spec((1, D_MODEL))],
        out_specs=pl.BlockSpec((tm, D_MODEL), lambda i: (i, 0)),
        out_shape=jax.ShapeDtypeStruct((t, D_MODEL), F32),
        compiler_params=_cparams(("parallel",)),
        name="mlp",
    )(x2, norm_g.reshape(1, D_MODEL), w1, w2, final_g.reshape(1, D_MODEL))


def _mix_weight(w_in):
    qkvz = w_in[:, 0:4 * MIX]
    gates = jnp.repeat(w_in[:, 4 * MIX:GDN_COLS], HEAD_DIM, axis=1)
    rest = w_in[:, GDN_COLS:GDN_COLS + LRU_COLS + S5_COLS + RWKV_COLS]
    return jnp.concatenate([qkvz, rest, gates], axis=1).astype(BF16)


def kernel(x, norm1_g, w_in, gdn_conv_w, gdn_a_log, gdn_dt_bias, gdn_norm_g, lru_conv_w, lru_conv_b, lru_w_a, lru_b_a, lru_w_x, lru_b_x, lru_lambda, s5_lambda_re, s5_lambda_im, s5_b_re, s5_b_im, s5_c_re, s5_c_im, s5_d, s5_log_dt, s5_glu_w, s5_glu_b, rwkv_mu, rwkv_w0, rwkv_w_up, rwkv_a0, rwkv_a_up, rwkv_g_up, rwkv_k_k, rwkv_k_a, rwkv_r_k, rwkv_ln_g, rwkv_ln_b, w_branch, w_out, norm2_g, mlp_w1, mlp_w2, final_norm_g):
    b, s, d = x.shape
    depth = w_in.shape[0]
    x2 = x.reshape(b * s, d)
    gate_off = GDN_COLS + LRU_COLS + S5_COLS + RWKV_COLS
    for l in range(depth):
        p_gdn, p_lru, p_s5, p_rwkv, p_ba = _in_proj(x2, norm1_g[l], _mix_weight(w_in[l]))
        shp = lambda t: t.reshape(b, s, t.shape[-1])
        ys = (
            _gdn(shp(p_gdn), shp(p_ba), gdn_conv_w[l], gdn_a_log[l], gdn_dt_bias[l], gdn_norm_g[l]),
            _lru(shp(p_lru), lru_conv_w[l], lru_conv_b[l], lru_w_a[l], lru_b_a[l], lru_w_x[l], lru_b_x[l],
                 lru_lambda[l]),
            _s5(shp(p_s5), s5_lambda_re[l], s5_lambda_im[l], s5_b_re[l], s5_b_im[l], s5_c_re[l], s5_c_im[l],
                s5_d[l], s5_log_dt[l], s5_glu_w[l], s5_glu_b[l]),
            _rwkv(shp(p_rwkv), rwkv_mu[l], rwkv_w0[l], rwkv_w_up[l], rwkv_a0[l], rwkv_a_up[l], rwkv_g_up[l],
                  rwkv_k_k[l], rwkv_k_a[l], rwkv_r_k[l], rwkv_ln_g[l], rwkv_ln_b[l]),
        )
        ys = tuple(y.reshape(b * s, MIX) for y in ys)
        x2 = _merge(x2, norm1_g[l], w_in[l][:, gate_off:].astype(BF16), ys,
                    w_branch[l].astype(BF16), w_out[l].astype(BF16))
        x2 = _mlp(x2, norm2_g[l], mlp_w1[l].astype(BF16), mlp_w2[l].astype(BF16), final_norm_g,
                  final_norm=(l == depth - 1))
    return x2.reshape(b, s, d)
```

```python
import functools
import math

import jax
import jax.numpy as jnp
from jax import lax
from jax.experimental import pallas as pl
from jax.experimental.pallas import tpu as pltpu

F32 = jnp.float32
BF16 = jnp.bfloat16

D_MODEL = 1024
MIX = 256
HEADS = 4
HEAD_DIM = 64
CHUNK = 64
CONV_W = 4
EPS = 1e-6
LRU_C = 8.0
S5_GROUPS = 16
S5_GROUP = 16
S5_STATE = 64
S5_LANES = S5_GROUPS * S5_STATE
DECAY_LORA = 64
AAA_LORA = 64
GATE_LORA = 128
RWKV_LN_EPS = 64e-5
D_FF = 4 * D_MODEL
LANE = 128
SUBLANE = 8
CONV_TAIL = SUBLANE

GDN_COLS = 4 * MIX + 2 * HEADS
LRU_COLS = 2 * MIX
S5_COLS = MIX
RWKV_COLS = 3 * MIX + DECAY_LORA + AAA_LORA + GATE_LORA
GDN_GATE_COLS = 2 * MIX
MIX_COLS = 4 * MIX + LRU_COLS + S5_COLS + RWKV_COLS + GDN_GATE_COLS

ROW_TILE = 512
TC_GDN = 256
TC_RWKV = 256
TC_LRU = 256
S5_BLOCK = 8
S5_NBLK = 256
VMEM_LIMIT = 56 * 1024 * 1024

_P_QK = 1
_P_A = 1
_P_INV = 1
_P_UW = 1
_P_STATE = 1

_NN = (((1,), (0,)), ((), ()))
_NT = (((1,), (1,)), ((), ()))
_TN = (((0,), (0,)), ((), ()))


def _mm(a, b, dims=_NN):
    return lax.dot_general(a, b, dims, preferred_element_type=F32)


def _mm1(a, b, dims=_NN):
    return _mm(a.astype(BF16), b.astype(BF16), dims)


def _split2(x):
    hi = x.astype(BF16)
    lo = (x - hi.astype(F32)).astype(BF16)
    return hi, lo


def _mm_exact_lhs(a_bf, x):
    x1 = x.astype(BF16)
    r1 = x - x1.astype(F32)
    x2 = r1.astype(BF16)
    x3 = (r1 - x2.astype(F32)).astype(BF16)
    return _mm(a_bf, x1) + (_mm(a_bf, x2) + _mm(a_bf, x3))


def _seg_sum(x, ones_bd):
    hi, lo = _split2(x)
    return _mm(hi, ones_bd) + _mm(lo, ones_bd)


def _iota2(shape, dim):
    return lax.broadcasted_iota(jnp.int32, shape, dim)


def _block_ones(n, shift):
    r = jnp.right_shift(_iota2((n, n), 0), shift)
    c = jnp.right_shift(_iota2((n, n), 1), shift)
    return jnp.where(r == c, 1.0, 0.0).astype(BF16)


def _block_tril(n, shift):
    ri = _iota2((n, n), 0)
    ci = _iota2((n, n), 1)
    same = jnp.right_shift(ri, shift) == jnp.right_shift(ci, shift)
    return jnp.where(same & (ci <= ri), 1.0, 0.0).astype(BF16)


def _softplus(x):
    return jnp.maximum(x, 0.0) + jnp.log1p(jnp.exp(-jnp.abs(x)))


def _sigmoid(x):
    return 1.0 / (1.0 + jnp.exp(-x))


def _silu(x):
    return x * _sigmoid(x)


def _gelu_tanh(x):
    c = math.sqrt(2.0 / math.pi)
    return 0.5 * x * (1.0 + jnp.tanh(c * (x + 0.044715 * (x * x * x))))


def _rms(x, g):
    return x * lax.rsqrt(jnp.mean(x * x, axis=-1, keepdims=True) + EPS) * g


PAIR = 2 * HEAD_DIM
PAIRS = (slice(0, PAIR), slice(PAIR, 2 * PAIR))


class _HeadMasks:
    def __init__(self):
        ri = _iota2((CHUNK, MIX), 0)
        ci = _iota2((CHUNK, MIX), 1) & (HEAD_DIM - 1)
        self.strict = ci < ri
        self.causal = ci <= ri
        self.upper = ri <= ci
        self.base = self.strict & (jnp.right_shift(ri, 3) == jnp.right_shift(ci, 3))
        self.levels = []
        for sh in (3, 4, 5):
            rb = jnp.right_shift(ri, sh)
            cb = jnp.right_shift(ci, sh)
            self.levels.append((rb == cb + 1) & ((rb & 1) == 1))
        self.eye = jnp.where(ri == ci, 1.0, 0.0).astype(F32)
        first = _iota2((CHUNK, PAIR), 1) < HEAD_DIM
        self.lane_lo = jnp.where(first, 1.0, 0.0).astype(BF16)
        self.lane_hi = jnp.where(first, 0.0, 1.0).astype(BF16)
        r2 = jnp.right_shift(_iota2((PAIR, PAIR), 0), 6)
        c2 = jnp.right_shift(_iota2((PAIR, PAIR), 1), 6)
        self.bd = r2 == c2
        self.ones_bd = _block_ones(MIX, 6)


def _embed(xb, hm):
    return jnp.concatenate([xb * hm.lane_lo, xb * hm.lane_hi], axis=0)


def _hmm(a, b, hm, dims=_NN, passes=3):
    m = a.shape[0]
    outs = []
    for s in PAIRS:
        if passes == 1:
            outs.append(_mm(a[:, s].astype(BF16), _embed(b[:, s].astype(BF16), hm), dims))
        else:
            ah, al = _split2(a[:, s])
            bh, bl = _split2(b[:, s])
            top = _mm(jnp.concatenate([ah, al], axis=0), _embed(bh, hm), dims)
            outs.append(top[:m] + (top[m:] + _mm(ah, _embed(bl, hm), dims)))
    return jnp.concatenate(outs, axis=1)


def _inv_unit_lower(lmats, hm):
    p = _P_INV
    ld = [jnp.where(hm.base, l, 0.0) for l in lmats]
    l2 = [_hmm(x, x, hm, _NN, p) for x in ld]
    l4 = [_hmm(x, x, hm, _NN, p) for x in l2]
    t = [hm.eye - x for x in ld]
    t = [ti + _hmm(ti, x, hm, _NN, p) for ti, x in zip(t, l2)]
    t = [ti + _hmm(ti, x, hm, _NN, p) for ti, x in zip(t, l4)]
    for m in hm.levels:
        off = [jnp.where(m, l, 0.0) for l in lmats]
        ta = [_hmm(ti, x, hm, _NN, p) for ti, x in zip(t, off)]
        t = [ti - _hmm(x, ti, hm, _NN, p) for ti, x in zip(t, ta)]
    return t


def _state_mm(lhs, state, dims=_NN):
    outs = []
    for s, st in zip(PAIRS, state):
        lh = lhs[:, s].astype(BF16)
        sh = st.astype(BF16)
        out = _mm(lh, sh, dims)
        if _P_STATE >= 2:
            out = out + _mm(lh, (st - sh.astype(F32)).astype(BF16), dims)
        if _P_STATE >= 3:
            out = out + _mm((lhs[:, s] - lh.astype(F32)).astype(BF16), sh, dims)
        outs.append(out)
    return jnp.concatenate(outs, axis=1)


def _state_update(state, decay_row, lhs, rhs, hm):
    new = []
    for s, st in zip(PAIRS, state):
        upd = _mm(lhs[:, s].astype(BF16), rhs[:, s].astype(BF16), _TN)
        new.append(st * decay_row[:, s] + jnp.where(hm.bd, upd, 0.0))
    return new


def _shift_rows(x, sh, fill, row):
    if sh % SUBLANE == 0:
        return jnp.concatenate([jnp.full((sh, x.shape[1]), fill, x.dtype), x[:x.shape[0] - sh]], axis=0)
    return jnp.where(row >= sh, pltpu.roll(x, sh, 0), fill)


def _causal_conv(xbuf, cur, w_ref, tc):
    xbuf[CONV_TAIL:CONV_TAIL + tc, :] = cur
    acc = None
    for i in range(CONV_W):
        off = CONV_TAIL - (CONV_W - 1) + i
        term = w_ref[i:i + 1, :] * xbuf[off:off + tc, :]
        acc = term if acc is None else acc + term
    xbuf[0:CONV_TAIL, :] = xbuf[tc:tc + CONV_TAIL, :]
    return acc


def _cparams(sem):
    return pltpu.CompilerParams(dimension_semantics=sem, vmem_limit_bytes=VMEM_LIMIT)


def _const_spec(shape):
    nd = len(shape)
    return pl.BlockSpec(shape, lambda *_: (0,) * nd)


def _layer_spec(shape, layer):
    nd = len(shape)
    return pl.BlockSpec((None,) + tuple(shape), lambda *_: (layer,) + (0,) * nd)


def _inproj_kernel(x_ref, g_ref, w_ref, gdn_ref, lru_ref, s5_ref, rwkv_ref, ba_ref):
    h = _rms(x_ref[...], g_ref[...]).astype(BF16)
    p = _mm(h, w_ref[...])
    o = 0
    for ref in (gdn_ref, lru_ref, s5_ref, rwkv_ref, ba_ref):
        wdt = ref.shape[-1]
        ref[...] = p[:, o:o + wdt]
        o += wdt


def _in_proj(x2, norm_g, w_mix, layer):
    t = x2.shape[0]
    tm = min(ROW_TILE, t)
    widths = (4 * MIX, LRU_COLS, S5_COLS, RWKV_COLS, GDN_GATE_COLS)
    return pl.pallas_call(
        _inproj_kernel,
        grid=(t // tm,),
        in_specs=[pl.BlockSpec((tm, D_MODEL), lambda i: (i, 0)),
                  _const_spec((1, D_MODEL)),
                  _layer_spec((D_MODEL, MIX_COLS), layer)],
        out_specs=[pl.BlockSpec((tm, w), lambda i: (i, 0)) for w in widths],
        out_shape=[jax.ShapeDtypeStruct((t, w), F32) for w in widths],
        compiler_params=_cparams(("parallel",)),
        name="in_proj",
    )(x2, norm_g.reshape(1, D_MODEL), w_mix)


def _gdn_kernel(p_ref, gl_ref, cw_ref, alog_ref, dtb_ref, ng_ref, y_ref, xbuf, s_scr, o_scr, *, tc, nb):
    @pl.when(pl.program_id(0) == 0)
    def _():
        xbuf[:, 0:CONV_TAIL, :] = jnp.zeros((nb, CONV_TAIL, 3 * MIX), F32)
        s_scr[...] = jnp.zeros_like(s_scr)

    hm = _HeadMasks()
    ones_bd = hm.ones_bd
    ones_cc = jnp.ones((CHUNK, CHUNK), BF16)
    tril = _block_tril(tc, 6)
    nc = tc // CHUNK

    items = []
    for bi in range(nb):
        qkv = _silu(_causal_conv(xbuf.at[bi], p_ref[bi, :, 0:3 * MIX], cw_ref, tc))
        q = qkv[:, 0:MIX]
        k = qkv[:, MIX:2 * MIX]
        v = qkv[:, 2 * MIX:3 * MIX]
        q = q * lax.rsqrt(_seg_sum(q * q, ones_bd) + EPS) * (HEAD_DIM ** -0.5)
        k = k * lax.rsqrt(_seg_sum(k * k, ones_bd) + EPS)
        beta = _sigmoid(gl_ref[bi, :, 0:MIX])
        g = -jnp.exp(alog_ref[...]) * _softplus(gl_ref[bi, :, MIX:2 * MIX] + dtb_ref[...])
        gc = _mm_exact_lhs(tril, g)
        for c in range(nc):
            rows = slice(c * CHUNK, (c + 1) * CHUNK)
            items.append(dict(bi=bi, c=c, q=q[rows], k=k[rows], v=v[rows], beta=beta[rows],
                              g=g[rows], gc=gc[rows]))

    for it in items:
        g2 = _mm_exact_lhs(ones_cc, jnp.where(hm.upper, it["g"], 0.0))
        diff = jnp.where(hm.causal, it["gc"] - g2, 0.0)
        it["dmat"] = jnp.where(hm.causal, jnp.exp(diff), 0.0)
        it["kb"] = it["k"] * it["beta"]
    for it in items:
        kk = _hmm(it["kb"], it["k"], hm, _NT, _P_QK)
        it["lmat"] = jnp.where(hm.strict, kk * it["dmat"], 0.0)
    tinvs = _inv_unit_lower([it["lmat"] for it in items], hm)
    for it, tinv in zip(items, tinvs):
        eg = jnp.exp(it["gc"])
        it["u"] = _hmm(tinv, it["v"] * it["beta"], hm, _NN, _P_UW)
        it["w"] = _hmm(tinv, it["kb"] * eg, hm, _NN, _P_UW)
        it["attn"] = _hmm(it["q"], it["k"], hm, _NT, _P_QK) * it["dmat"]
        it["qd"] = it["q"] * eg
        glast = it["gc"][CHUNK - 1:CHUNK, :]
        it["kdec"] = it["k"] * jnp.exp(glast - it["gc"])
        it["elast"] = jnp.exp(glast)

    states = [[s_scr[bi, pi] for pi in range(len(PAIRS))] for bi in range(nb)]
    for c in range(nc):
        for bi in range(nb):
            it = items[bi * nc + c]
            ws = _state_mm(jnp.concatenate([it["w"], it["qd"]], axis=0), states[bi])
            vnew = it["u"] - ws[:CHUNK]
            o = ws[CHUNK:] + _hmm(it["attn"], vnew, hm, _NN, 1)
            states[bi] = _state_update(states[bi], it["elast"], it["kdec"], vnew, hm)
            o_scr[bi, c * CHUNK:(c + 1) * CHUNK, :] = o
    for bi in range(nb):
        for pi in range(len(PAIRS)):
            s_scr[bi, pi] = states[bi][pi]

    for bi in range(nb):
        o = o_scr[bi]
        o = o * lax.rsqrt(_seg_sum(o * o, ones_bd) * (1.0 / HEAD_DIM) + EPS) * ng_ref[...]
        y_ref[bi] = (o * _silu(p_ref[bi, :, 3 * MIX:4 * MIX])).astype(y_ref.dtype)


def _head_row(t):
    return jnp.repeat(t, HEAD_DIM).reshape(1, MIX)


def _gdn(p, gl, conv_w, a_log, dt_bias, norm_g):
    b, s, _ = p.shape
    tc = min(TC_GDN, s)
    ng_row = jnp.tile(norm_g, HEADS).reshape(1, MIX)
    return pl.pallas_call(
        functools.partial(_gdn_kernel, tc=tc, nb=b),
        grid=(s // tc,),
        in_specs=[pl.BlockSpec((b, tc, 4 * MIX), lambda j: (0, j, 0)),
                  pl.BlockSpec((b, tc, GDN_GATE_COLS), lambda j: (0, j, 0)),
                  _const_spec((CONV_W, 3 * MIX)),
                  _const_spec((1, MIX)), _const_spec((1, MIX)), _const_spec((1, MIX))],
        out_specs=pl.BlockSpec((b, tc, MIX), lambda j: (0, j, 0)),
        out_shape=jax.ShapeDtypeStruct((b, s, MIX), BF16),
        scratch_shapes=[pltpu.VMEM((b, tc + CONV_TAIL, 3 * MIX), F32),
                        pltpu.VMEM((b, len(PAIRS), PAIR, PAIR), F32),
                        pltpu.VMEM((b, tc, MIX), F32)],
        compiler_params=_cparams(("arbitrary",)),
        name="gdn_mixer",
    )(p, gl, conv_w, _head_row(a_log), _head_row(dt_bias), ng_row)


def _lru_kernel(p_ref, cw_ref, cb_ref, wa_ref, ba_ref, wx_ref, bx_ref, lam_ref, y_ref, xbuf, hcar, *, tc):
    @pl.when(pl.program_id(1) == 0)
    def _():
        xbuf[0:CONV_TAIL, :] = jnp.zeros((CONV_TAIL, MIX), F32)
        hcar[...] = jnp.zeros_like(hcar)

    u = _causal_conv(xbuf, p_ref[:, 0:MIX], cw_ref, tc) + cb_ref[...]
    ub = u.astype(BF16)
    r = _sigmoid(_mm(ub, wa_ref[...]) + ba_ref[...])
    i = _sigmoid(_mm(ub, wx_ref[...]) + bx_ref[...])
    log_a = (-LRU_C) * r * _softplus(-lam_ref[...])
    a = jnp.exp(log_a)
    inp = jnp.sqrt(-jnp.tanh(log_a) * (a * a + 1.0)) * (i * u)

    row = _iota2((tc, MIX), 0)
    acc_a, acc_b = a, inp
    sh = 1
    while sh < tc:
        a_sh = _shift_rows(acc_a, sh, 1.0, row)
        b_sh = _shift_rows(acc_b, sh, 0.0, row)
        acc_b = acc_b + acc_a * b_sh
        acc_a = acc_a * a_sh
        sh *= 2
    h = acc_b + acc_a * hcar[...]
    hcar[...] = h[tc - 1:tc, :]
    y_ref[...] = (h * _gelu_tanh(p_ref[:, MIX:2 * MIX])).astype(y_ref.dtype)


def _block_diag(w):
    n, d, e = w.shape
    eye = jnp.eye(n, dtype=w.dtype)
    return (eye[:, None, :, None] * w[:, :, None, :]).reshape(n * d, n * e)


def _lru(p, conv_w, conv_b, w_a, b_a, w_x, b_x, lam):
    b, s, _ = p.shape
    tc = min(TC_LRU, s)
    row = lambda t: t.reshape(1, MIX)
    return pl.pallas_call(
        functools.partial(_lru_kernel, tc=tc),
        grid=(b, s // tc),
        in_specs=[pl.BlockSpec((None, tc, LRU_COLS), lambda i, j: (i, j, 0)),
                  _const_spec((CONV_W, MIX)), _const_spec((1, MIX)),
                  _const_spec((MIX, MIX)), _const_spec((1, MIX)),
                  _const_spec((MIX, MIX)), _const_spec((1, MIX)), _const_spec((1, MIX))],
        out_specs=pl.BlockSpec((None, tc, MIX), lambda i, j: (i, j, 0)),
        out_shape=jax.ShapeDtypeStruct((b, s, MIX), BF16),
        scratch_shapes=[pltpu.VMEM((tc + CONV_TAIL, MIX), F32), pltpu.VMEM((1, MIX), F32)],
        compiler_params=_cparams(("parallel", "arbitrary")),
        name="rglru_mixer",
    )(p, conv_w, row(conv_b), _block_diag(w_a).astype(BF16), row(b_a),
      _block_diag(w_x).astype(BF16), row(b_x), row(lam))


def _s5_kernel(u_ref, e_ref, k_ref, p_ref, lstep_ref, lpow_ref, d_ref, gw_ref, gb_ref, y_ref, hcar, *, nblk):
    @pl.when(pl.program_id(1) == 0)
    def _():
        hcar[...] = jnp.zeros_like(hcar)

    ub = [u_ref[:, j * MIX:(j + 1) * MIX].astype(BF16) for j in range(S5_BLOCK)]
    hr = hi = None
    for j in range(S5_BLOCK):
        tr = _mm(ub[j], e_ref[j])
        ti = _mm(ub[j], e_ref[S5_BLOCK + j])
        hr, hi = (tr, ti) if hr is None else (hr + tr, hi + ti)

    row = _iota2((nblk, S5_LANES), 0)
    sh, kk = 1, 0
    while sh < nblk:
        sr = _shift_rows(hr, sh, 0.0, row)
        si = _shift_rows(hi, sh, 0.0, row)
        lr = lstep_ref[kk:kk + 1, 0:S5_LANES]
        li = lstep_ref[kk:kk + 1, S5_LANES:2 * S5_LANES]
        hr, hi = hr + (lr * sr - li * si), hi + (lr * si + li * sr)
        sh *= 2
        kk += 1
    cr = hcar[:, 0:S5_LANES]
    ci = hcar[:, S5_LANES:2 * S5_LANES]
    pr = lpow_ref[:, 0:S5_LANES]
    pi = lpow_ref[:, S5_LANES:2 * S5_LANES]
    hr, hi = hr + (pr * cr - pi * ci), hi + (pr * ci + pi * cr)
    hcar[:, 0:S5_LANES] = hr[nblk - 1:nblk, :]
    hcar[:, S5_LANES:2 * S5_LANES] = hi[nblk - 1:nblk, :]
    first = row == 0
    hr = jnp.where(first, cr, pltpu.roll(hr, 1, 0))
    hi = jnp.where(first, ci, pltpu.roll(hi, 1, 0))

    hr_hi, hr_lo = _split2(hr)
    hi_hi, hi_lo = _split2(hi)
    for i in range(S5_BLOCK):
        pr_i = p_ref[i]
        pi_i = p_ref[S5_BLOCK + i]
        y = (_mm(hr_hi, pr_i) + _mm(hi_hi, pi_i)) + (_mm(hr_lo, pr_i) + _mm(hi_lo, pi_i))
        for j in range(i + 1):
            y = y + _mm(ub[j], k_ref[i - j])
        y = _gelu_tanh(y + d_ref[...] * u_ref[:, i * MIX:(i + 1) * MIX])
        y = y * _sigmoid(_mm1(y, gw_ref[...]) + gb_ref[...])
        y_ref[:, i * MIX:(i + 1) * MIX] = y.astype(y_ref.dtype)


def _cmul(ar, ai, br, bi):
    return ar * br - ai * bi, ar * bi + ai * br


def _expand_kernel(x_ref, o_ref, *, row_shift, col_shift):
    rows, w = x_ref.shape
    cols = o_ref.shape[-1]
    sel = (_iota2((w, cols), 1) & (w - 1)) == _iota2((w, cols), 0)
    y = _mm(x_ref[...].astype(BF16), jnp.where(sel, 1.0, 0.0).astype(BF16))
    keep = (jnp.right_shift(_iota2((rows, cols), 0), row_shift)
            == jnp.right_shift(_iota2((rows, cols), 1), col_shift))
    o_ref[...] = jnp.where(keep, y, 0.0).astype(o_ref.dtype)


def _group_expand(x, row_shift, col_shift):
    m, rows, w = x.shape
    cols = S5_GROUPS * w
    return pl.pallas_call(
        functools.partial(_expand_kernel, row_shift=row_shift, col_shift=col_shift),
        grid=(m,),
        in_specs=[pl.BlockSpec((None, rows, w), lambda i: (i, 0, 0))],
        out_specs=pl.BlockSpec((None, rows, cols), lambda i: (i, 0, 0)),
        out_shape=jax.ShapeDtypeStruct((m, rows, cols), BF16),
        compiler_params=_cparams(("parallel",)),
        name="s5_expand",
    )(x)


def _s5_params(lam_re, lam_im, b_re, b_im, c_re, c_im, log_dt, nblk):
    hp = lax.Precision.HIGHEST
    n = S5_BLOCK
    dt = jnp.exp(log_dt)[:, None]
    mag = jnp.exp(lam_re * dt)
    lbr = mag * jnp.cos(lam_im * dt)
    lbi = mag * jnp.sin(lam_im * dt)
    den = lam_re * lam_re + lam_im * lam_im
    fr = ((lbr - 1.0) * lam_re + lbi * lam_im) / den
    fi = (lbi * lam_re - (lbr - 1.0) * lam_im) / den
    bbr = fr[..., None] * b_re - fi[..., None] * b_im
    bbi = fr[..., None] * b_im + fi[..., None] * b_re

    pws = [(jnp.ones_like(lbr), jnp.zeros_like(lbr))]
    for _ in range(n):
        pws.append(_cmul(pws[-1][0], pws[-1][1], lbr, lbi))
    pwr = jnp.stack([p[0] for p in pws])
    pwi = jnp.stack([p[1] for p in pws])
    clr, cli = _cmul(c_re[None], c_im[None], pwr[:, :, None, :], pwi[:, :, None, :])

    kmat = (jnp.einsum("kgcp,gpd->kgdc", clr[:n], bbr, precision=hp)
            - jnp.einsum("kgcp,gpd->kgdc", cli[:n], bbi, precision=hp))
    kmat = kmat.reshape(n, MIX, S5_GROUP)

    rr, ri = pwr[n - 1::-1][:, :, None, :], pwi[n - 1::-1][:, :, None, :]
    btr, bti = jnp.transpose(bbr, (0, 2, 1))[None], jnp.transpose(bbi, (0, 2, 1))[None]
    er, ei = _cmul(rr, ri, btr, bti)
    emat = jnp.concatenate([er, ei], axis=0).reshape(2 * n, MIX, S5_STATE)

    pcr = jnp.transpose(clr[1:n + 1], (0, 1, 3, 2))
    pci = -jnp.transpose(cli[1:n + 1], (0, 1, 3, 2))
    pmat = jnp.concatenate([pcr, pci], axis=0).reshape(2 * n, S5_LANES, S5_GROUP)

    pr = pwr[n].reshape(1, S5_LANES)
    pi = pwi[n].reshape(1, S5_LANES)
    steps = []
    m = 1
    while m < nblk:
        fr_, fi_ = pr[m - 1:m], pi[m - 1:m]
        steps.append(jnp.concatenate([fr_, fi_], axis=1))
        pr, pi = (jnp.concatenate([pr, pr * fr_ - pi * fi_], axis=0),
                  jnp.concatenate([pi, pr * fi_ + pi * fr_], axis=0))
        m *= 2
    lstep = jnp.concatenate(steps, axis=0)
    lpow = jnp.concatenate([pr, pi], axis=1)
    return emat, kmat, pmat, lstep, lpow


def _s5_nblk(s):
    return min(S5_NBLK, s // S5_BLOCK)


def _s5_prepare(lam_re, lam_im, b_re, b_im, c_re, c_im, log_dt, nblk):
    depth = lam_re.shape[0]
    per_layer = [_s5_params(lam_re[l], lam_im[l], b_re[l], b_im[l], c_re[l], c_im[l], log_dt[l], nblk)
                 for l in range(depth)]
    emat, kmat, pmat, lstep, lpow = (jnp.stack(t) for t in zip(*per_layer))

    def expand(t, row_shift, col_shift):
        out = _group_expand(t.reshape((-1,) + t.shape[2:]), row_shift, col_shift)
        return out.reshape((depth, -1) + out.shape[1:])

    return expand(emat, 4, 6), expand(kmat, 4, 4), expand(pmat, 6, 4), lstep, lpow


def _s5(u, prep, layer, d, glu_w, glu_b):
    b, s, _ = u.shape
    n = S5_BLOCK
    nb_tot = s // n
    nblk = _s5_nblk(s)
    emat, kbd, pmat, lstep, lpow = prep
    nstep = lstep.shape[1]
    blk = pl.BlockSpec((None, nblk, n * MIX), lambda i, j: (i, j, 0))
    out = pl.pallas_call(
        functools.partial(_s5_kernel, nblk=nblk),
        grid=(b, nb_tot // nblk),
        in_specs=[blk,
                  _layer_spec((2 * n, MIX, S5_LANES), layer),
                  _layer_spec((n, MIX, MIX), layer),
                  _layer_spec((2 * n, S5_LANES, MIX), layer),
                  _layer_spec((nstep, 2 * S5_LANES), layer),
                  _layer_spec((nblk, 2 * S5_LANES), layer),
                  _const_spec((1, MIX)), _const_spec((MIX, MIX)), _const_spec((1, MIX))],
        out_specs=blk,
        out_shape=jax.ShapeDtypeStruct((b, nb_tot, n * MIX), BF16),
        scratch_shapes=[pltpu.VMEM((1, 2 * S5_LANES), F32)],
        compiler_params=_cparams(("parallel", "arbitrary")),
        name="s5_mixer",
    )(u.reshape(b, nb_tot, n * MIX), emat, kbd, pmat, lstep, lpow,
      d.reshape(1, MIX), glu_w.astype(BF16), glu_b.reshape(1, MIX))
    return out.reshape(b, s, MIX)


def _rwkv_kernel(p_ref, mu_ref, w0_ref, wup_ref, a0_ref, aup_ref, gup_ref, kk_ref, ka_ref, rk_ref,
                 lng_ref, lnb_ref, y_ref, prev, s_scr, o_scr, *, tc, nb):
    @pl.when(pl.program_id(0) == 0)
    def _():
        prev[...] = jnp.zeros_like(prev)
        s_scr[...] = jnp.zeros_like(s_scr)

    hm = _HeadMasks()
    ones_bd = hm.ones_bd
    tril = _block_tril(tc, 6)
    nc = tc // CHUNK
    row = _iota2((tc, RWKV_COLS), 0)

    items = []
    post = []
    for bi in range(nb):
        p = p_ref[bi]
        shifted = jnp.where(row == 0, prev[bi], pltpu.roll(p, 1, 0))
        prev[bi] = p[tc - 1:tc, :]
        p = p + mu_ref[...] * (shifted - p)
        r = p[:, 0:MIX]
        k = p[:, MIX:2 * MIX]
        v = p[:, 2 * MIX:3 * MIX]
        o0 = 3 * MIX
        wd = p[:, o0:o0 + DECAY_LORA]
        ad = p[:, o0 + DECAY_LORA:o0 + DECAY_LORA + AAA_LORA]
        gd = p[:, o0 + DECAY_LORA + AAA_LORA:RWKV_COLS]

        logw = -_softplus(-(w0_ref[...] + _mm1(jnp.tanh(wd), wup_ref[...]))) - 0.5
        ld = -jnp.exp(logw)
        a = _sigmoid(a0_ref[...] + _mm1(ad, aup_ref[...]))
        g = _mm1(_sigmoid(gd), gup_ref[...])
        kk = k * kk_ref[...]
        kk = kk * lax.rsqrt(_seg_sum(kk * kk, ones_bd) + EPS)
        k = k * (1.0 + (a - 1.0) * ka_ref[...])
        akk = a * kk

        cum = _mm_exact_lhs(tril, ld)
        ncum = jnp.exp(-cum)
        r_t = r * jnp.exp(cum)
        b_t = kk * jnp.exp(cum - ld)
        k_t = k * ncum
        a_t = akk * ncum
        post.append((r, k, v, g))
        for c in range(nc):
            rows = slice(c * CHUNK, (c + 1) * CHUNK)
            clast = cum[(c + 1) * CHUNK - 1:(c + 1) * CHUNK, :]
            tail = jnp.exp(clast - cum[rows])
            items.append(dict(r=r_t[rows], b=b_t[rows], k=k_t[rows], a=a_t[rows], v=v[rows],
                              kend=k[rows] * tail, aend=akk[rows] * tail, plast=jnp.exp(clast)))

    for it in items:
        lhs = jnp.concatenate([it["b"], it["r"]], axis=0)
        x1 = _hmm(lhs, it["a"], hm, _NT, _P_A)
        x2 = _hmm(lhs, it["k"], hm, _NT, _P_A)
        it["a_ba"] = jnp.where(hm.strict, x1[:CHUNK], 0.0)
        it["a_ra"] = jnp.where(hm.causal, x1[CHUNK:], 0.0)
        a_bk = jnp.where(hm.strict, x2[:CHUNK], 0.0)
        a_rk = jnp.where(hm.causal, x2[CHUNK:], 0.0)
        av = _hmm(jnp.concatenate([a_bk, a_rk], axis=0), it["v"], hm, _NN, _P_A)
        it["abkv"] = av[:CHUNK]
        it["arkv"] = av[CHUNK:]
    tinvs = _inv_unit_lower([it["a_ba"] for it in items], hm)
    for it, tinv in zip(items, tinvs):
        it["tb"] = _hmm(tinv, it["b"], hm, _NN, _P_UW)
        it["tz0"] = _hmm(tinv, it["abkv"], hm, _NN, _P_UW)
        it["kaend"] = jnp.concatenate([it["kend"], it["aend"]], axis=0)

    states = [[s_scr[bi, pi] for pi in range(len(PAIRS))] for bi in range(nb)]
    for c in range(nc):
        for bi in range(nb):
            it = items[bi * nc + c]
            xs = _state_mm(jnp.concatenate([it["tb"], it["r"]], axis=0), states[bi], _NT)
            z = xs[:CHUNK] + it["tz0"]
            o = xs[CHUNK:] + it["arkv"] - _hmm(it["a_ra"], z, hm, _NN, _P_A)
            states[bi] = _state_update(states[bi], it["plast"], jnp.concatenate([it["v"], -z], axis=0),
                                       it["kaend"], hm)
            o_scr[bi, c * CHUNK:(c + 1) * CHUNK, :] = o
    for bi in range(nb):
        for pi in range(len(PAIRS)):
            s_scr[bi, pi] = states[bi][pi]

    inv_n = 1.0 / HEAD_DIM
    for bi in range(nb):
        r, k, v, g = post[bi]
        o = o_scr[bi]
        mean = _seg_sum(o, ones_bd) * inv_n
        cen = o - mean
        var = _seg_sum(cen * cen, ones_bd) * inv_n
        o = cen * lax.rsqrt(var + RWKV_LN_EPS) * lng_ref[...] + lnb_ref[...]
        bonus = _seg_sum(r * k * rk_ref[...], ones_bd) * v
        y_ref[bi] = ((o + bonus) * g).astype(y_ref.dtype)


def _rwkv(p, mu, w0, w_up, a0, a_up, g_up, k_k, k_a, r_k, ln_g, ln_b):
    b, s, _ = p.shape
    tc = min(TC_RWKV, s)
    row = lambda t: t.reshape(1, MIX)
    return pl.pallas_call(
        functools.partial(_rwkv_kernel, tc=tc, nb=b),
        grid=(s // tc,),
        in_specs=[pl.BlockSpec((b, tc, RWKV_COLS), lambda j: (0, j, 0)),
                  _const_spec((1, RWKV_COLS)),
                  _const_spec((1, MIX)), _const_spec((DECAY_LORA, MIX)),
                  _const_spec((1, MIX)), _const_spec((AAA_LORA, MIX)),
                  _const_spec((GATE_LORA, MIX)),
                  _const_spec((1, MIX)), _const_spec((1, MIX)), _const_spec((1, MIX)),
                  _const_spec((1, MIX)), _const_spec((1, MIX))],
        out_specs=pl.BlockSpec((b, tc, MIX), lambda j: (0, j, 0)),
        out_shape=jax.ShapeDtypeStruct((b, s, MIX), BF16),
        scratch_shapes=[pltpu.VMEM((b, 1, RWKV_COLS), F32),
                        pltpu.VMEM((b, len(PAIRS), PAIR, PAIR), F32),
                        pltpu.VMEM((b, tc, MIX), F32)],
        compiler_params=_cparams(("arbitrary",)),
        name="rwkv7_mixer",
    )(p, mu.reshape(1, RWKV_COLS), row(w0), w_up.astype(BF16), row(a0), a_up.astype(BF16),
      g_up.astype(BF16), row(k_k), row(k_a), row(r_k.reshape(MIX)), row(ln_g), row(ln_b))


def _merge_kernel(x_ref, g_ref, wg_ref, y0_ref, y1_ref, y2_ref, y3_ref, wb_ref, wo_ref, o_ref):
    x = x_ref[...]
    h = _rms(x, g_ref[...]).astype(BF16)
    merged = None
    for i, y_ref in enumerate((y0_ref, y1_ref, y2_ref, y3_ref)):
        gate = _sigmoid(_mm(h, wg_ref[:, i * D_MODEL:(i + 1) * D_MODEL]))
        term = gate * _mm(y_ref[...], wb_ref[i])
        merged = term if merged is None else merged + term
    o_ref[...] = x + _mm(merged.astype(BF16), wo_ref[...])


def _merge(x2, norm_g, w_gate, ys, w_branch, w_out, layer):
    t = x2.shape[0]
    tm = min(ROW_TILE, t)
    return pl.pallas_call(
        _merge_kernel,
        grid=(t // tm,),
        in_specs=[pl.BlockSpec((tm, D_MODEL), lambda i: (i, 0)),
                  _const_spec((1, D_MODEL)),
                  _layer_spec((D_MODEL, HEADS * D_MODEL), layer)]
                 + [pl.BlockSpec((tm, MIX), lambda i: (i, 0)) for _ in range(4)]
                 + [_layer_spec((4, MIX, D_MODEL), layer), _layer_spec((D_MODEL, D_MODEL), layer)],
        out_specs=pl.BlockSpec((tm, D_MODEL), lambda i: (i, 0)),
        out_shape=jax.ShapeDtypeStruct((t, D_MODEL), F32),
        compiler_params=_cparams(("parallel",)),
        name="gated_merge",
    )(x2, norm_g.reshape(1, D_MODEL), w_gate, *ys, w_branch, w_out)


def _mlp_kernel(x_ref, g_ref, w1_ref, w2_ref, gf_ref, o_ref, *, final_norm):
    x = x_ref[...]
    h = _rms(x, g_ref[...]).astype(BF16)
    a = jnp.maximum(_mm(h, w1_ref[...]), 0.0)
    x = x + _mm((a * a).astype(BF16), w2_ref[...])
    if final_norm:
        x = _rms(x, gf_ref[...])
    o_ref[...] = x


def _mlp(x2, norm_g, w1, w2, final_g, final_norm, layer):
    t = x2.shape[0]
    tm = min(ROW_TILE, t)
    return pl.pallas_call(
        functools.partial(_mlp_kernel, final_norm=final_norm),
        grid=(t // tm,),
        in_specs=[pl.BlockSpec((tm, D_MODEL), lambda i: (i, 0)),
                  _const_spec((1, D_MODEL)),
                  _layer_spec((D_MODEL, D_FF), layer), _layer_spec((D_FF, D_MODEL), layer),
                  _const_spec((1, D_MODEL))],
        out_specs=pl.BlockSpec((tm, D_MODEL), lambda i: (i, 0)),
        out_shape=jax.ShapeDtypeStruct((t, D_MODEL), F32),
        compiler_params=_cparams(("parallel",)),
        name="mlp",
    )(x2, norm_g.reshape(1, D_MODEL), w1, w2, final_g.reshape(1, D_MODEL))


def _mix_weight(w_in):
    qkvz = w_in[:, 0:4 * MIX]
    gates = jnp.repeat(w_in[:, 4 * MIX:GDN_COLS], HEAD_DIM, axis=1)
    rest = w_in[:, GDN_COLS:GDN_COLS + LRU_COLS + S5_COLS + RWKV_COLS]
    return jnp.concatenate([qkvz, rest, gates], axis=1).astype(BF16)


def kernel(x, norm1_g, w_in, gdn_conv_w, gdn_a_log, gdn_dt_bias, gdn_norm_g, lru_conv_w, lru_conv_b, lru_w_a, lru_b_a, lru_w_x, lru_b_x, lru_lambda, s5_lambda_re, s5_lambda_im, s5_b_re, s5_b_im, s5_c_re, s5_c_im, s5_d, s5_log_dt, s5_glu_w, s5_glu_b, rwkv_mu, rwkv_w0, rwkv_w_up, rwkv_a0, rwkv_a_up, rwkv_g_up, rwkv_k_k, rwkv_k_a, rwkv_r_k, rwkv_ln_g, rwkv_ln_b, w_branch, w_out, norm2_g, mlp_w1, mlp_w2, final_norm_g):
    b, s, d = x.shape
    depth = w_in.shape[0]
    x2 = x.reshape(b * s, d)
    gate_off = GDN_COLS + LRU_COLS + S5_COLS + RWKV_COLS
    w_mix = jax.vmap(_mix_weight)(w_in)
    w_gate = w_in[:, :, gate_off:].astype(BF16)
    w_branch, w_out = w_branch.astype(BF16), w_out.astype(BF16)
    mlp_w1, mlp_w2 = mlp_w1.astype(BF16), mlp_w2.astype(BF16)
    s5_prep = _s5_prepare(s5_lambda_re, s5_lambda_im, s5_b_re, s5_b_im, s5_c_re, s5_c_im, s5_log_dt,
                          _s5_nblk(s))
    for l in range(depth):
        p_gdn, p_lru, p_s5, p_rwkv, p_ba = _in_proj(x2, norm1_g[l], w_mix, l)
        shp = lambda t: t.reshape(b, s, t.shape[-1])
        ys = (
            _gdn(shp(p_gdn), shp(p_ba), gdn_conv_w[l], gdn_a_log[l], gdn_dt_bias[l], gdn_norm_g[l]),
            _lru(shp(p_lru), lru_conv_w[l], lru_conv_b[l], lru_w_a[l], lru_b_a[l], lru_w_x[l], lru_b_x[l],
                 lru_lambda[l]),
            _s5(shp(p_s5), s5_prep, l, s5_d[l], s5_glu_w[l], s5_glu_b[l]),
            _rwkv(shp(p_rwkv), rwkv_mu[l], rwkv_w0[l], rwkv_w_up[l], rwkv_a0[l], rwkv_a_up[l], rwkv_g_up[l],
                  rwkv_k_k[l], rwkv_k_a[l], rwkv_r_k[l], rwkv_ln_g[l], rwkv_ln_b[l]),
        )
        ys = tuple(y.reshape(b * s, MIX) for y in ys)
        x2 = _merge(x2, norm1_g[l], w_gate, ys, w_branch, w_out, l)
        x2 = _mlp(x2, norm2_g[l], mlp_w1, mlp_w2, final_norm_g, final_norm=(l == depth - 1), layer=l)
    return x2.reshape(b, s, d)
```

```python
import functools
import math

import jax
import jax.numpy as jnp
from jax import lax
from jax.experimental import pallas as pl
from jax.experimental.pallas import tpu as pltpu

F32 = jnp.float32
BF16 = jnp.bfloat16

D_MODEL = 1024
MIX = 256
HEADS = 4
HEAD_DIM = 64
CHUNK = 64
CONV_W = 4
EPS = 1e-6
LRU_C = 8.0
S5_GROUPS = 16
S5_GROUP = 16
S5_STATE = 64
S5_LANES = S5_GROUPS * S5_STATE
DECAY_LORA = 64
AAA_LORA = 64
GATE_LORA = 128
RWKV_LN_EPS = 64e-5
D_FF = 4 * D_MODEL
LANE = 128
SUBLANE = 8
CONV_TAIL = SUBLANE

GDN_COLS = 4 * MIX + 2 * HEADS
LRU_COLS = 2 * MIX
S5_COLS = MIX
RWKV_COLS = 3 * MIX + DECAY_LORA + AAA_LORA + GATE_LORA
GDN_GATE_COLS = 2 * MIX
MIX_COLS = 4 * MIX + LRU_COLS + S5_COLS + RWKV_COLS + GDN_GATE_COLS

ROW_TILE = 512
TC_GDN = 256
TC_RWKV = 256
TC_LRU = 256
S5_BLOCK = 8
S5_NBLK = 256
VMEM_LIMIT = 56 * 1024 * 1024

_P_QK = 1
_P_A = 1
_P_INV = 1
_P_UW = 1
_P_STATE = 1

_NN = (((1,), (0,)), ((), ()))
_NT = (((1,), (1,)), ((), ()))
_TN = (((0,), (0,)), ((), ()))


def _mm(a, b, dims=_NN):
    return lax.dot_general(a, b, dims, preferred_element_type=F32)


def _mm1(a, b, dims=_NN):
    return _mm(a.astype(BF16), b.astype(BF16), dims)


def _split2(x):
    hi = x.astype(BF16)
    lo = (x - hi.astype(F32)).astype(BF16)
    return hi, lo


def _mm_exact_lhs(a_bf, x):
    x1 = x.astype(BF16)
    r1 = x - x1.astype(F32)
    x2 = r1.astype(BF16)
    x3 = (r1 - x2.astype(F32)).astype(BF16)
    return _mm(a_bf, x1) + (_mm(a_bf, x2) + _mm(a_bf, x3))


def _seg_sum(x, ones_bd):
    hi, lo = _split2(x)
    return _mm(hi, ones_bd) + _mm(lo, ones_bd)


def _iota2(shape, dim):
    return lax.broadcasted_iota(jnp.int32, shape, dim)


def _block_ones(n, shift):
    r = jnp.right_shift(_iota2((n, n), 0), shift)
    c = jnp.right_shift(_iota2((n, n), 1), shift)
    return jnp.where(r == c, 1.0, 0.0).astype(BF16)


def _block_tril(n, shift):
    ri = _iota2((n, n), 0)
    ci = _iota2((n, n), 1)
    same = jnp.right_shift(ri, shift) == jnp.right_shift(ci, shift)
    return jnp.where(same & (ci <= ri), 1.0, 0.0).astype(BF16)


def _softplus(x):
    return jnp.maximum(x, 0.0) + jnp.log1p(jnp.exp(-jnp.abs(x)))


def _sigmoid(x):
    return 1.0 / (1.0 + jnp.exp(-x))


def _silu(x):
    return x * _sigmoid(x)


def _gelu_tanh(x):
    c = math.sqrt(2.0 / math.pi)
    return 0.5 * x * (1.0 + jnp.tanh(c * (x + 0.044715 * (x * x * x))))


def _rms(x, g):
    return x * lax.rsqrt(jnp.mean(x * x, axis=-1, keepdims=True) + EPS) * g


PAIR = 2 * HEAD_DIM
PAIRS = (slice(0, PAIR), slice(PAIR, 2 * PAIR))


class _HeadMasks:
    def __init__(self):
        ri = _iota2((CHUNK, MIX), 0)
        ci = _iota2((CHUNK, MIX), 1) & (HEAD_DIM - 1)
        self.strict = ci < ri
        self.causal = ci <= ri
        self.upper = ri <= ci
        self.base = self.strict & (jnp.right_shift(ri, 3) == jnp.right_shift(ci, 3))
        self.levels = []
        for sh in (3, 4, 5):
            rb = jnp.right_shift(ri, sh)
            cb = jnp.right_shift(ci, sh)
            self.levels.append((rb == cb + 1) & ((rb & 1) == 1))
        self.eye = jnp.where(ri == ci, 1.0, 0.0).astype(F32)
        first = _iota2((CHUNK, PAIR), 1) < HEAD_DIM
        self.lane_lo = jnp.where(first, 1.0, 0.0).astype(BF16)
        self.lane_hi = jnp.where(first, 0.0, 1.0).astype(BF16)
        r2 = jnp.right_shift(_iota2((PAIR, PAIR), 0), 6)
        c2 = jnp.right_shift(_iota2((PAIR, PAIR), 1), 6)
        self.bd = r2 == c2
        self.ones_bd = _block_ones(MIX, 6)


def _embed(xb, hm):
    return jnp.concatenate([xb * hm.lane_lo, xb * hm.lane_hi], axis=0)


def _hmm(a, b, hm, dims=_NN, passes=3):
    m = a.shape[0]
    outs = []
    for s in PAIRS:
        if passes == 1:
            outs.append(_mm(a[:, s].astype(BF16), _embed(b[:, s].astype(BF16), hm), dims))
        else:
            ah, al = _split2(a[:, s])
            bh, bl = _split2(b[:, s])
            top = _mm(jnp.concatenate([ah, al], axis=0), _embed(bh, hm), dims)
            outs.append(top[:m] + (top[m:] + _mm(ah, _embed(bl, hm), dims)))
    return jnp.concatenate(outs, axis=1)


def _inv_unit_lower(lmats, hm):
    p = _P_INV
    ld = [jnp.where(hm.base, l, 0.0) for l in lmats]
    l2 = [_hmm(x, x, hm, _NN, p) for x in ld]
    l4 = [_hmm(x, x, hm, _NN, p) for x in l2]
    t = [hm.eye - x for x in ld]
    t = [ti + _hmm(ti, x, hm, _NN, p) for ti, x in zip(t, l2)]
    t = [ti + _hmm(ti, x, hm, _NN, p) for ti, x in zip(t, l4)]
    for m in hm.levels:
        off = [jnp.where(m, l, 0.0) for l in lmats]
        ta = [_hmm(ti, x, hm, _NN, p) for ti, x in zip(t, off)]
        t = [ti - _hmm(x, ti, hm, _NN, p) for ti, x in zip(t, ta)]
    return t


def _state_mm(lhs, state, dims=_NN):
    outs = []
    for s, st in zip(PAIRS, state):
        lh = lhs[:, s].astype(BF16)
        sh = st.astype(BF16)
        out = _mm(lh, sh, dims)
        if _P_STATE >= 2:
            out = out + _mm(lh, (st - sh.astype(F32)).astype(BF16), dims)
        if _P_STATE >= 3:
            out = out + _mm((lhs[:, s] - lh.astype(F32)).astype(BF16), sh, dims)
        outs.append(out)
    return jnp.concatenate(outs, axis=1)


def _state_update(state, decay_row, lhs, rhs, hm):
    new = []
    for s, st in zip(PAIRS, state):
        upd = _mm(lhs[:, s].astype(BF16), rhs[:, s].astype(BF16), _TN)
        new.append(st * decay_row[:, s] + jnp.where(hm.bd, upd, 0.0))
    return new


def _shift_rows(x, sh, fill, row):
    if sh % SUBLANE == 0:
        return jnp.concatenate([jnp.full((sh, x.shape[1]), fill, x.dtype), x[:x.shape[0] - sh]], axis=0)
    return jnp.where(row >= sh, pltpu.roll(x, sh, 0), fill)


def _causal_conv(xbuf, cur, w_ref, tc):
    xbuf[CONV_TAIL:CONV_TAIL + tc, :] = cur
    acc = None
    for i in range(CONV_W):
        off = CONV_TAIL - (CONV_W - 1) + i
        term = w_ref[i:i + 1, :] * xbuf[off:off + tc, :]
        acc = term if acc is None else acc + term
    xbuf[0:CONV_TAIL, :] = xbuf[tc:tc + CONV_TAIL, :]
    return acc


def _cparams(sem):
    return pltpu.CompilerParams(dimension_semantics=sem, vmem_limit_bytes=VMEM_LIMIT)


def _const_spec(shape):
    nd = len(shape)
    return pl.BlockSpec(shape, lambda *_: (0,) * nd)


def _layer_spec(shape, layer):
    nd = len(shape)
    return pl.BlockSpec((None,) + tuple(shape), lambda *_: (layer,) + (0,) * nd)


def _inproj_kernel(x_ref, g_ref, w_ref, gdn_ref, lru_ref, s5_ref, rwkv_ref, ba_ref):
    h = _rms(x_ref[...], g_ref[...]).astype(BF16)
    p = _mm(h, w_ref[...])
    o = 0
    for ref in (gdn_ref, lru_ref, s5_ref, rwkv_ref, ba_ref):
        wdt = ref.shape[-1]
        ref[...] = p[:, o:o + wdt]
        o += wdt


def _in_proj(x2, norm_g, w_mix, layer):
    t = x2.shape[0]
    tm = min(ROW_TILE, t)
    widths = (4 * MIX, LRU_COLS, S5_COLS, RWKV_COLS, GDN_GATE_COLS)
    return pl.pallas_call(
        _inproj_kernel,
        grid=(t // tm,),
        in_specs=[pl.BlockSpec((tm, D_MODEL), lambda i: (i, 0)),
                  _const_spec((1, D_MODEL)),
                  _layer_spec((D_MODEL, MIX_COLS), layer)],
        out_specs=[pl.BlockSpec((tm, w), lambda i: (i, 0)) for w in widths],
        out_shape=[jax.ShapeDtypeStruct((t, w), F32) for w in widths],
        compiler_params=_cparams(("parallel",)),
        name="in_proj",
    )(x2, norm_g.reshape(1, D_MODEL), w_mix)


def _gdn_kernel(p_ref, gl_ref, cw_ref, alog_ref, dtb_ref, ng_ref, y_ref, xbuf, s_scr, o_scr, *, tc, nb):
    @pl.when(pl.program_id(0) == 0)
    def _():
        xbuf[:, 0:CONV_TAIL, :] = jnp.zeros((nb, CONV_TAIL, 3 * MIX), F32)
        s_scr[...] = jnp.zeros_like(s_scr)

    hm = _HeadMasks()
    ones_bd = hm.ones_bd
    ones_cc = jnp.ones((CHUNK, CHUNK), BF16)
    tril = _block_tril(tc, 6)
    nc = tc // CHUNK

    items = []
    for bi in range(nb):
        qkv = _silu(_causal_conv(xbuf.at[bi], p_ref[bi, :, 0:3 * MIX], cw_ref, tc))
        q = qkv[:, 0:MIX]
        k = qkv[:, MIX:2 * MIX]
        v = qkv[:, 2 * MIX:3 * MIX]
        q = q * lax.rsqrt(_seg_sum(q * q, ones_bd) + EPS) * (HEAD_DIM ** -0.5)
        k = k * lax.rsqrt(_seg_sum(k * k, ones_bd) + EPS)
        beta = _sigmoid(gl_ref[bi, :, 0:MIX])
        g = -jnp.exp(alog_ref[...]) * _softplus(gl_ref[bi, :, MIX:2 * MIX] + dtb_ref[...])
        gc = _mm_exact_lhs(tril, g)
        for c in range(nc):
            rows = slice(c * CHUNK, (c + 1) * CHUNK)
            items.append(dict(bi=bi, c=c, q=q[rows], k=k[rows], v=v[rows], beta=beta[rows],
                              g=g[rows], gc=gc[rows]))

    for it in items:
        g2 = _mm_exact_lhs(ones_cc, jnp.where(hm.upper, it["g"], 0.0))
        diff = jnp.where(hm.causal, it["gc"] - g2, 0.0)
        it["dmat"] = jnp.where(hm.causal, jnp.exp(diff), 0.0)
        it["kb"] = it["k"] * it["beta"]
    for it in items:
        kk = _hmm(it["kb"], it["k"], hm, _NT, _P_QK)
        it["lmat"] = jnp.where(hm.strict, kk * it["dmat"], 0.0)
    tinvs = _inv_unit_lower([it["lmat"] for it in items], hm)
    for it, tinv in zip(items, tinvs):
        eg = jnp.exp(it["gc"])
        it["u"] = _hmm(tinv, it["v"] * it["beta"], hm, _NN, _P_UW)
        it["w"] = _hmm(tinv, it["kb"] * eg, hm, _NN, _P_UW)
        it["attn"] = _hmm(it["q"], it["k"], hm, _NT, _P_QK) * it["dmat"]
        it["qd"] = it["q"] * eg
        glast = it["gc"][CHUNK - 1:CHUNK, :]
        it["kdec"] = it["k"] * jnp.exp(glast - it["gc"])
        it["elast"] = jnp.exp(glast)

    states = [[s_scr[bi, pi] for pi in range(len(PAIRS))] for bi in range(nb)]
    for c in range(nc):
        for bi in range(nb):
            it = items[bi * nc + c]
            ws = _state_mm(jnp.concatenate([it["w"], it["qd"]], axis=0), states[bi])
            vnew = it["u"] - ws[:CHUNK]
            o = ws[CHUNK:] + _hmm(it["attn"], vnew, hm, _NN, 1)
            states[bi] = _state_update(states[bi], it["elast"], it["kdec"], vnew, hm)
            o_scr[bi, c * CHUNK:(c + 1) * CHUNK, :] = o
    for bi in range(nb):
        for pi in range(len(PAIRS)):
            s_scr[bi, pi] = states[bi][pi]

    for bi in range(nb):
        o = o_scr[bi]
        o = o * lax.rsqrt(_seg_sum(o * o, ones_bd) * (1.0 / HEAD_DIM) + EPS) * ng_ref[...]
        y_ref[bi] = (o * _silu(p_ref[bi, :, 3 * MIX:4 * MIX])).astype(y_ref.dtype)


def _head_row(t):
    return jnp.repeat(t, HEAD_DIM).reshape(1, MIX)


def _gdn(p, gl, conv_w, a_log, dt_bias, norm_g):
    b, s, _ = p.shape
    tc = min(TC_GDN, s)
    ng_row = jnp.tile(norm_g, HEADS).reshape(1, MIX)
    return pl.pallas_call(
        functools.partial(_gdn_kernel, tc=tc, nb=b),
        grid=(s // tc,),
        in_specs=[pl.BlockSpec((b, tc, 4 * MIX), lambda j: (0, j, 0)),
                  pl.BlockSpec((b, tc, GDN_GATE_COLS), lambda j: (0, j, 0)),
                  _const_spec((CONV_W, 3 * MIX)),
                  _const_spec((1, MIX)), _const_spec((1, MIX)), _const_spec((1, MIX))],
        out_specs=pl.BlockSpec((b, tc, MIX), lambda j: (0, j, 0)),
        out_shape=jax.ShapeDtypeStruct((b, s, MIX), BF16),
        scratch_shapes=[pltpu.VMEM((b, tc + CONV_TAIL, 3 * MIX), F32),
                        pltpu.VMEM((b, len(PAIRS), PAIR, PAIR), F32),
                        pltpu.VMEM((b, tc, MIX), F32)],
        compiler_params=_cparams(("arbitrary",)),
        name="gdn_mixer",
    )(p, gl, conv_w, _head_row(a_log), _head_row(dt_bias), ng_row)


def _lru_kernel(p_ref, cw_ref, cb_ref, wa_ref, ba_ref, wx_ref, bx_ref, lam_ref, y_ref, xbuf, hcar, *, tc):
    @pl.when(pl.program_id(1) == 0)
    def _():
        xbuf[0:CONV_TAIL, :] = jnp.zeros((CONV_TAIL, MIX), F32)
        hcar[...] = jnp.zeros_like(hcar)

    u = _causal_conv(xbuf, p_ref[:, 0:MIX], cw_ref, tc) + cb_ref[...]
    ub = u.astype(BF16)
    r = _sigmoid(_mm(ub, wa_ref[...]) + ba_ref[...])
    i = _sigmoid(_mm(ub, wx_ref[...]) + bx_ref[...])
    log_a = (-LRU_C) * r * _softplus(-lam_ref[...])
    a = jnp.exp(log_a)
    inp = jnp.sqrt(-jnp.tanh(log_a) * (a * a + 1.0)) * (i * u)

    row = _iota2((tc, MIX), 0)
    acc_a, acc_b = a, inp
    sh = 1
    while sh < tc:
        a_sh = _shift_rows(acc_a, sh, 1.0, row)
        b_sh = _shift_rows(acc_b, sh, 0.0, row)
        acc_b = acc_b + acc_a * b_sh
        acc_a = acc_a * a_sh
        sh *= 2
    h = acc_b + acc_a * hcar[...]
    hcar[...] = h[tc - 1:tc, :]
    y_ref[...] = (h * _gelu_tanh(p_ref[:, MIX:2 * MIX])).astype(y_ref.dtype)


def _block_diag(w):
    n, d, e = w.shape
    eye = jnp.eye(n, dtype=w.dtype)
    return (eye[:, None, :, None] * w[:, :, None, :]).reshape(n * d, n * e)


def _lru(p, conv_w, conv_b, w_a, b_a, w_x, b_x, lam):
    b, s, _ = p.shape
    tc = min(TC_LRU, s)
    row = lambda t: t.reshape(1, MIX)
    return pl.pallas_call(
        functools.partial(_lru_kernel, tc=tc),
        grid=(b, s // tc),
        in_specs=[pl.BlockSpec((None, tc, LRU_COLS), lambda i, j: (i, j, 0)),
                  _const_spec((CONV_W, MIX)), _const_spec((1, MIX)),
                  _const_spec((MIX, MIX)), _const_spec((1, MIX)),
                  _const_spec((MIX, MIX)), _const_spec((1, MIX)), _const_spec((1, MIX))],
        out_specs=pl.BlockSpec((None, tc, MIX), lambda i, j: (i, j, 0)),
        out_shape=jax.ShapeDtypeStruct((b, s, MIX), BF16),
        scratch_shapes=[pltpu.VMEM((tc + CONV_TAIL, MIX), F32), pltpu.VMEM((1, MIX), F32)],
        compiler_params=_cparams(("parallel", "arbitrary")),
        name="rglru_mixer",
    )(p, conv_w, row(conv_b), _block_diag(w_a).astype(BF16), row(b_a),
      _block_diag(w_x).astype(BF16), row(b_x), row(lam))


def _s5_kernel(u_ref, e_ref, k_ref, p_ref, lstep_ref, lpow_ref, d_ref, gw_ref, gb_ref, y_ref, hcar, *, nblk):
    @pl.when(pl.program_id(1) == 0)
    def _():
        hcar[...] = jnp.zeros_like(hcar)

    ub = [u_ref[:, j * MIX:(j + 1) * MIX].astype(BF16) for j in range(S5_BLOCK)]
    hr = hi = None
    for j in range(S5_BLOCK):
        tr = _mm(ub[j], e_ref[j])
        ti = _mm(ub[j], e_ref[S5_BLOCK + j])
        hr, hi = (tr, ti) if hr is None else (hr + tr, hi + ti)

    row = _iota2((nblk, S5_LANES), 0)
    sh, kk = 1, 0
    while sh < nblk:
        sr = _shift_rows(hr, sh, 0.0, row)
        si = _shift_rows(hi, sh, 0.0, row)
        lr = lstep_ref[kk:kk + 1, 0:S5_LANES]
        li = lstep_ref[kk:kk + 1, S5_LANES:2 * S5_LANES]
        hr, hi = hr + (lr * sr - li * si), hi + (lr * si + li * sr)
        sh *= 2
        kk += 1
    cr = hcar[:, 0:S5_LANES]
    ci = hcar[:, S5_LANES:2 * S5_LANES]
    pr = lpow_ref[:, 0:S5_LANES]
    pi = lpow_ref[:, S5_LANES:2 * S5_LANES]
    hr, hi = hr + (pr * cr - pi * ci), hi + (pr * ci + pi * cr)
    hcar[:, 0:S5_LANES] = hr[nblk - 1:nblk, :]
    hcar[:, S5_LANES:2 * S5_LANES] = hi[nblk - 1:nblk, :]
    first = row == 0
    hr = jnp.where(first, cr, pltpu.roll(hr, 1, 0))
    hi = jnp.where(first, ci, pltpu.roll(hi, 1, 0))

    n = S5_BLOCK
    hr_hi, hr_lo = _split2(hr)
    hi_hi, hi_lo = _split2(hi)
    p_re, p_im = p_ref[0], p_ref[1]
    ys = (_mm(hr_hi, p_re) + _mm(hi_hi, p_im)) + (_mm(hr_lo, p_re) + _mm(hi_lo, p_im))
    acc = [ys[:, i * MIX:(i + 1) * MIX] for i in range(n)]
    ustack = jnp.concatenate(ub, axis=0)
    for k in range(n):
        out = _mm(ustack[0:(n - k) * nblk], k_ref[k])
        for j in range(n - k):
            acc[j + k] = acc[j + k] + out[j * nblk:(j + 1) * nblk]
    y = jnp.concatenate([acc[i] + d_ref[...] * u_ref[:, i * MIX:(i + 1) * MIX] for i in range(n)], axis=0)
    y = _gelu_tanh(y)
    y = (y * _sigmoid(_mm1(y, gw_ref[...]) + gb_ref[...])).astype(y_ref.dtype)
    for i in range(n):
        y_ref[:, i * MIX:(i + 1) * MIX] = y[i * nblk:(i + 1) * nblk]


def _cmul(ar, ai, br, bi):
    return ar * br - ai * bi, ar * bi + ai * br


def _expand_kernel(x_ref, o_ref, *, row_shift, col_shift):
    rows, w = x_ref.shape
    cols = o_ref.shape[-1]
    sel = (_iota2((w, cols), 1) & (w - 1)) == _iota2((w, cols), 0)
    y = _mm(x_ref[...].astype(BF16), jnp.where(sel, 1.0, 0.0).astype(BF16))
    keep = (jnp.right_shift(_iota2((rows, cols), 0), row_shift)
            == jnp.right_shift(_iota2((rows, cols), 1), col_shift))
    o_ref[...] = jnp.where(keep, y, 0.0).astype(o_ref.dtype)


def _group_expand(x, row_shift, col_shift):
    m, rows, w = x.shape
    cols = S5_GROUPS * w
    return pl.pallas_call(
        functools.partial(_expand_kernel, row_shift=row_shift, col_shift=col_shift),
        grid=(m,),
        in_specs=[pl.BlockSpec((None, rows, w), lambda i: (i, 0, 0))],
        out_specs=pl.BlockSpec((None, rows, cols), lambda i: (i, 0, 0)),
        out_shape=jax.ShapeDtypeStruct((m, rows, cols), BF16),
        compiler_params=_cparams(("parallel",)),
        name="s5_expand",
    )(x)


def _s5_params(lam_re, lam_im, b_re, b_im, c_re, c_im, log_dt, nblk):
    hp = lax.Precision.HIGHEST
    n = S5_BLOCK
    dt = jnp.exp(log_dt)[:, None]
    mag = jnp.exp(lam_re * dt)
    lbr = mag * jnp.cos(lam_im * dt)
    lbi = mag * jnp.sin(lam_im * dt)
    den = lam_re * lam_re + lam_im * lam_im
    fr = ((lbr - 1.0) * lam_re + lbi * lam_im) / den
    fi = (lbi * lam_re - (lbr - 1.0) * lam_im) / den
    bbr = fr[..., None] * b_re - fi[..., None] * b_im
    bbi = fr[..., None] * b_im + fi[..., None] * b_re

    pws = [(jnp.ones_like(lbr), jnp.zeros_like(lbr))]
    for _ in range(n):
        pws.append(_cmul(pws[-1][0], pws[-1][1], lbr, lbi))
    pwr = jnp.stack([p[0] for p in pws])
    pwi = jnp.stack([p[1] for p in pws])
    clr, cli = _cmul(c_re[None], c_im[None], pwr[:, :, None, :], pwi[:, :, None, :])

    kmat = (jnp.einsum("kgcp,gpd->kgdc", clr[:n], bbr, precision=hp)
            - jnp.einsum("kgcp,gpd->kgdc", cli[:n], bbi, precision=hp))
    kmat = kmat.reshape(n, MIX, S5_GROUP)

    rr, ri = pwr[n - 1::-1][:, :, None, :], pwi[n - 1::-1][:, :, None, :]
    btr, bti = jnp.transpose(bbr, (0, 2, 1))[None], jnp.transpose(bbi, (0, 2, 1))[None]
    er, ei = _cmul(rr, ri, btr, bti)
    emat = jnp.concatenate([er, ei], axis=0).reshape(2 * n, MIX, S5_STATE)

    pcr = jnp.transpose(clr[1:n + 1], (0, 1, 3, 2))
    pci = -jnp.transpose(cli[1:n + 1], (0, 1, 3, 2))
    pmat = jnp.concatenate([pcr, pci], axis=0).reshape(2 * n, S5_LANES, S5_GROUP)

    pr = pwr[n].reshape(1, S5_LANES)
    pi = pwi[n].reshape(1, S5_LANES)
    steps = []
    m = 1
    while m < nblk:
        fr_, fi_ = pr[m - 1:m], pi[m - 1:m]
        steps.append(jnp.concatenate([fr_, fi_], axis=1))
        pr, pi = (jnp.concatenate([pr, pr * fr_ - pi * fi_], axis=0),
                  jnp.concatenate([pi, pr * fi_ + pi * fr_], axis=0))
        m *= 2
    lstep = jnp.concatenate(steps, axis=0)
    lpow = jnp.concatenate([pr, pi], axis=1)
    return emat, kmat, pmat, lstep, lpow


def _s5_nblk(s):
    return min(S5_NBLK, s // S5_BLOCK)


def _s5_prepare(lam_re, lam_im, b_re, b_im, c_re, c_im, log_dt, nblk):
    depth = lam_re.shape[0]
    per_layer = [_s5_params(lam_re[l], lam_im[l], b_re[l], b_im[l], c_re[l], c_im[l], log_dt[l], nblk)
                 for l in range(depth)]
    emat, kmat, pmat, lstep, lpow = (jnp.stack(t) for t in zip(*per_layer))

    def expand(t, row_shift, col_shift):
        out = _group_expand(t.reshape((-1,) + t.shape[2:]), row_shift, col_shift)
        return out.reshape((depth, -1) + out.shape[1:])

    pfull = expand(pmat, 6, 4)
    pfull = pfull.reshape(depth, 2, S5_BLOCK, S5_LANES, MIX).transpose(0, 1, 3, 2, 4)
    pfull = pfull.reshape(depth, 2, S5_LANES, S5_BLOCK * MIX)
    return expand(emat, 4, 6), expand(kmat, 4, 4), pfull, lstep, lpow


def _s5(u, prep, layer, d, glu_w, glu_b):
    b, s, _ = u.shape
    n = S5_BLOCK
    nb_tot = s // n
    nblk = _s5_nblk(s)
    emat, kbd, pmat, lstep, lpow = prep
    nstep = lstep.shape[1]
    blk = pl.BlockSpec((None, nblk, n * MIX), lambda i, j: (i, j, 0))
    out = pl.pallas_call(
        functools.partial(_s5_kernel, nblk=nblk),
        grid=(b, nb_tot // nblk),
        in_specs=[blk,
                  _layer_spec((2 * n, MIX, S5_LANES), layer),
                  _layer_spec((n, MIX, MIX), layer),
                  _layer_spec((2, S5_LANES, n * MIX), layer),
                  _layer_spec((nstep, 2 * S5_LANES), layer),
                  _layer_spec((nblk, 2 * S5_LANES), layer),
                  _const_spec((1, MIX)), _const_spec((MIX, MIX)), _const_spec((1, MIX))],
        out_specs=blk,
        out_shape=jax.ShapeDtypeStruct((b, nb_tot, n * MIX), BF16),
        scratch_shapes=[pltpu.VMEM((1, 2 * S5_LANES), F32)],
        compiler_params=_cparams(("parallel", "arbitrary")),
        name="s5_mixer",
    )(u.reshape(b, nb_tot, n * MIX), emat, kbd, pmat, lstep, lpow,
      d.reshape(1, MIX), glu_w.astype(BF16), glu_b.reshape(1, MIX))
    return out.reshape(b, s, MIX)


def _rwkv_kernel(p_ref, mu_ref, w0_ref, wup_ref, a0_ref, aup_ref, gup_ref, kk_ref, ka_ref, rk_ref,
                 lng_ref, lnb_ref, y_ref, prev, s_scr, o_scr, *, tc, nb):
    @pl.when(pl.program_id(0) == 0)
    def _():
        prev[...] = jnp.zeros_like(prev)
        s_scr[...] = jnp.zeros_like(s_scr)

    hm = _HeadMasks()
    ones_bd = hm.ones_bd
    tril = _block_tril(tc, 6)
    nc = tc // CHUNK
    row = _iota2((tc, RWKV_COLS), 0)

    items = []
    post = []
    for bi in range(nb):
        p = p_ref[bi]
        shifted = jnp.where(row == 0, prev[bi], pltpu.roll(p, 1, 0))
        prev[bi] = p[tc - 1:tc, :]
        p = p + mu_ref[...] * (shifted - p)
        r = p[:, 0:MIX]
        k = p[:, MIX:2 * MIX]
        v = p[:, 2 * MIX:3 * MIX]
        o0 = 3 * MIX
        wd = p[:, o0:o0 + DECAY_LORA]
        ad = p[:, o0 + DECAY_LORA:o0 + DECAY_LORA + AAA_LORA]
        gd = p[:, o0 + DECAY_LORA + AAA_LORA:RWKV_COLS]

        logw = -_softplus(-(w0_ref[...] + _mm1(jnp.tanh(wd), wup_ref[...]))) - 0.5
        ld = -jnp.exp(logw)
        a = _sigmoid(a0_ref[...] + _mm1(ad, aup_ref[...]))
        g = _mm1(_sigmoid(gd), gup_ref[...])
        kk = k * kk_ref[...]
        kk = kk * lax.rsqrt(_seg_sum(kk * kk, ones_bd) + EPS)
        k = k * (1.0 + (a - 1.0) * ka_ref[...])
        akk = a * kk

        cum = _mm_exact_lhs(tril, ld)
        ncum = jnp.exp(-cum)
        r_t = r * jnp.exp(cum)
        b_t = kk * jnp.exp(cum - ld)
        k_t = k * ncum
        a_t = akk * ncum
        post.append((r, k, v, g))
        for c in range(nc):
            rows = slice(c * CHUNK, (c + 1) * CHUNK)
            clast = cum[(c + 1) * CHUNK - 1:(c + 1) * CHUNK, :]
            tail = jnp.exp(clast - cum[rows])
            items.append(dict(r=r_t[rows], b=b_t[rows], k=k_t[rows], a=a_t[rows], v=v[rows],
                              kend=k[rows] * tail, aend=akk[rows] * tail, plast=jnp.exp(clast)))

    for it in items:
        lhs = jnp.concatenate([it["b"], it["r"]], axis=0)
        x1 = _hmm(lhs, it["a"], hm, _NT, _P_A)
        x2 = _hmm(lhs, it["k"], hm, _NT, _P_A)
        it["a_ba"] = jnp.where(hm.strict, x1[:CHUNK], 0.0)
        it["a_ra"] = jnp.where(hm.causal, x1[CHUNK:], 0.0)
        a_bk = jnp.where(hm.strict, x2[:CHUNK], 0.0)
        a_rk = jnp.where(hm.causal, x2[CHUNK:], 0.0)
        av = _hmm(jnp.concatenate([a_bk, a_rk], axis=0), it["v"], hm, _NN, _P_A)
        it["abkv"] = av[:CHUNK]
        it["arkv"] = av[CHUNK:]
    tinvs = _inv_unit_lower([it["a_ba"] for it in items], hm)
    for it, tinv in zip(items, tinvs):
        it["tb"] = _hmm(tinv, it["b"], hm, _NN, _P_UW)
        it["tz0"] = _hmm(tinv, it["abkv"], hm, _NN, _P_UW)
        it["kaend"] = jnp.concatenate([it["kend"], it["aend"]], axis=0)

    states = [[s_scr[bi, pi] for pi in range(len(PAIRS))] for bi in range(nb)]
    for c in range(nc):
        for bi in range(nb):
            it = items[bi * nc + c]
            xs = _state_mm(jnp.concatenate([it["tb"], it["r"]], axis=0), states[bi], _NT)
            z = xs[:CHUNK] + it["tz0"]
            o = xs[CHUNK:] + it["arkv"] - _hmm(it["a_ra"], z, hm, _NN, _P_A)
            states[bi] = _state_update(states[bi], it["plast"], jnp.concatenate([it["v"], -z], axis=0),
                                       it["kaend"], hm)
            o_scr[bi, c * CHUNK:(c + 1) * CHUNK, :] = o
    for bi in range(nb):
        for pi in range(len(PAIRS)):
            s_scr[bi, pi] = states[bi][pi]

    inv_n = 1.0 / HEAD_DIM
    for bi in range(nb):
        r, k, v, g = post[bi]
        o = o_scr[bi]
        mean = _seg_sum(o, ones_bd) * inv_n
        cen = o - mean
        var = _seg_sum(cen * cen, ones_bd) * inv_n
        o = cen * lax.rsqrt(var + RWKV_LN_EPS) * lng_ref[...] + lnb_ref[...]
        bonus = _seg_sum(r * k * rk_ref[...], ones_bd) * v
        y_ref[bi] = ((o + bonus) * g).astype(y_ref.dtype)


def _rwkv(p, mu, w0, w_up, a0, a_up, g_up, k_k, k_a, r_k, ln_g, ln_b):
    b, s, _ = p.shape
    tc = min(TC_RWKV, s)
    row = lambda t: t.reshape(1, MIX)
    return pl.pallas_call(
        functools.partial(_rwkv_kernel, tc=tc, nb=b),
        grid=(s // tc,),
        in_specs=[pl.BlockSpec((b, tc, RWKV_COLS), lambda j: (0, j, 0)),
                  _const_spec((1, RWKV_COLS)),
                  _const_spec((1, MIX)), _const_spec((DECAY_LORA, MIX)),
                  _const_spec((1, MIX)), _const_spec((AAA_LORA, MIX)),
                  _const_spec((GATE_LORA, MIX)),
                  _const_spec((1, MIX)), _const_spec((1, MIX)), _const_spec((1, MIX)),
                  _const_spec((1, MIX)), _const_spec((1, MIX))],
        out_specs=pl.BlockSpec((b, tc, MIX), lambda j: (0, j, 0)),
        out_shape=jax.ShapeDtypeStruct((b, s, MIX), BF16),
        scratch_shapes=[pltpu.VMEM((b, 1, RWKV_COLS), F32),
                        pltpu.VMEM((b, len(PAIRS), PAIR, PAIR), F32),
                        pltpu.VMEM((b, tc, MIX), F32)],
        compiler_params=_cparams(("arbitrary",)),
        name="rwkv7_mixer",
    )(p, mu.reshape(1, RWKV_COLS), row(w0), w_up.astype(BF16), row(a0), a_up.astype(BF16),
      g_up.astype(BF16), row(k_k), row(k_a), row(r_k.reshape(MIX)), row(ln_g), row(ln_b))


def _merge_kernel(x_ref, g_ref, wg_ref, y0_ref, y1_ref, y2_ref, y3_ref, wb_ref, wo_ref, o_ref):
    x = x_ref[...]
    h = _rms(x, g_ref[...]).astype(BF16)
    merged = None
    for i, y_ref in enumerate((y0_ref, y1_ref, y2_ref, y3_ref)):
        gate = _sigmoid(_mm(h, wg_ref[:, i * D_MODEL:(i + 1) * D_MODEL]))
        term = gate * _mm(y_ref[...], wb_ref[i])
        merged = term if merged is None else merged + term
    o_ref[...] = x + _mm(merged.astype(BF16), wo_ref[...])


def _merge(x2, norm_g, w_gate, ys, w_branch, w_out, layer):
    t = x2.shape[0]
    tm = min(ROW_TILE, t)
    return pl.pallas_call(
        _merge_kernel,
        grid=(t // tm,),
        in_specs=[pl.BlockSpec((tm, D_MODEL), lambda i: (i, 0)),
                  _const_spec((1, D_MODEL)),
                  _layer_spec((D_MODEL, HEADS * D_MODEL), layer)]
                 + [pl.BlockSpec((tm, MIX), lambda i: (i, 0)) for _ in range(4)]
                 + [_layer_spec((4, MIX, D_MODEL), layer), _layer_spec((D_MODEL, D_MODEL), layer)],
        out_specs=pl.BlockSpec((tm, D_MODEL), lambda i: (i, 0)),
        out_shape=jax.ShapeDtypeStruct((t, D_MODEL), F32),
        compiler_params=_cparams(("parallel",)),
        name="gated_merge",
    )(x2, norm_g.reshape(1, D_MODEL), w_gate, *ys, w_branch, w_out)


def _mlp_kernel(x_ref, g_ref, w1_ref, w2_ref, gf_ref, o_ref, *, final_norm):
    x = x_ref[...]
    h = _rms(x, g_ref[...]).astype(BF16)
    a = jnp.maximum(_mm(h, w1_ref[...]), 0.0)
    x = x + _mm((a * a).astype(BF16), w2_ref[...])
    if final_norm:
        x = _rms(x, gf_ref[...])
    o_ref[...] = x


def _mlp(x2, norm_g, w1, w2, final_g, final_norm, layer):
    t = x2.shape[0]
    tm = min(ROW_TILE, t)
    return pl.pallas_call(
        functools.partial(_mlp_kernel, final_norm=final_norm),
        grid=(t // tm,),
        in_specs=[pl.BlockSpec((tm, D_MODEL), lambda i: (i, 0)),
                  _const_spec((1, D_MODEL)),
                  _layer_spec((D_MODEL, D_FF), layer), _layer_spec((D_FF, D_MODEL), layer),
                  _const_spec((1, D_MODEL))],
        out_specs=pl.BlockSpec((tm, D_MODEL), lambda i: (i, 0)),
        out_shape=jax.ShapeDtypeStruct((t, D_MODEL), F32),
        compiler_params=_cparams(("parallel",)),
        name="mlp",
    )(x2, norm_g.reshape(1, D_MODEL), w1, w2, final_g.reshape(1, D_MODEL))


def _mix_weight(w_in):
    qkvz = w_in[:, 0:4 * MIX]
    gates = jnp.repeat(w_in[:, 4 * MIX:GDN_COLS], HEAD_DIM, axis=1)
    rest = w_in[:, GDN_COLS:GDN_COLS + LRU_COLS + S5_COLS + RWKV_COLS]
    return jnp.concatenate([qkvz, rest, gates], axis=1).astype(BF16)


def kernel(x, norm1_g, w_in, gdn_conv_w, gdn_a_log, gdn_dt_bias, gdn_norm_g, lru_conv_w, lru_conv_b, lru_w_a, lru_b_a, lru_w_x, lru_b_x, lru_lambda, s5_lambda_re, s5_lambda_im, s5_b_re, s5_b_im, s5_c_re, s5_c_im, s5_d, s5_log_dt, s5_glu_w, s5_glu_b, rwkv_mu, rwkv_w0, rwkv_w_up, rwkv_a0, rwkv_a_up, rwkv_g_up, rwkv_k_k, rwkv_k_a, rwkv_r_k, rwkv_ln_g, rwkv_ln_b, w_branch, w_out, norm2_g, mlp_w1, mlp_w2, final_norm_g):
    b, s, d = x.shape
    depth = w_in.shape[0]
    x2 = x.reshape(b * s, d)
    gate_off = GDN_COLS + LRU_COLS + S5_COLS + RWKV_COLS
    w_mix = jax.vmap(_mix_weight)(w_in)
    w_gate = w_in[:, :, gate_off:].astype(BF16)
    w_branch, w_out = w_branch.astype(BF16), w_out.astype(BF16)
    mlp_w1, mlp_w2 = mlp_w1.astype(BF16), mlp_w2.astype(BF16)
    s5_prep = _s5_prepare(s5_lambda_re, s5_lambda_im, s5_b_re, s5_b_im, s5_c_re, s5_c_im, s5_log_dt,
                          _s5_nblk(s))
    for l in range(depth):
        p_gdn, p_lru, p_s5, p_rwkv, p_ba = _in_proj(x2, norm1_g[l], w_mix, l)
        shp = lambda t: t.reshape(b, s, t.shape[-1])
        ys = (
            _gdn(shp(p_gdn), shp(p_ba), gdn_conv_w[l], gdn_a_log[l], gdn_dt_bias[l], gdn_norm_g[l]),
            _lru(shp(p_lru), lru_conv_w[l], lru_conv_b[l], lru_w_a[l], lru_b_a[l], lru_w_x[l], lru_b_x[l],
                 lru_lambda[l]),
            _s5(shp(p_s5), s5_prep, l, s5_d[l], s5_glu_w[l], s5_glu_b[l]),
            _rwkv(shp(p_rwkv), rwkv_mu[l], rwkv_w0[l], rwkv_w_up[l], rwkv_a0[l], rwkv_a_up[l], rwkv_g_up[l],
                  rwkv_k_k[l], rwkv_k_a[l], rwkv_r_k[l], rwkv_ln_g[l], rwkv_ln_b[l]),
        )
        ys = tuple(y.reshape(b * s, MIX) for y in ys)
        x2 = _merge(x2, norm1_g[l], w_gate, ys, w_branch, w_out, l)
        x2 = _mlp(x2, norm2_g[l], mlp_w1, mlp_w2, final_norm_g, final_norm=(l == depth - 1), layer=l)
    return x2.reshape(b, s, d)
```
